```python
import jax, jax.numpy as jnp
from jax import lax
import numpy as np

D_MODEL = 4096
BATCH = 2
SEQ = 4096
DEPTH = 2
DEC_BATCH = 16
DEC_SEQ = 32
PAST_LEN = 2048

CHUNK = 64
D_MIX = D_MODEL
GROUP_WIDTH = D_MIX // 4
A_WIDTH = GROUP_WIDTH
B_WIDTH = GROUP_WIDTH
C_WIDTH = GROUP_WIDTH
D_WIDTH = GROUP_WIDTH
A_CHUNK = 128
A_HEAD_DIM = 128
A_HEADS = A_WIDTH // A_HEAD_DIM
B_KERNEL = 3
C_KERNEL = 31
C_HEAD_DIM = 128
C_HEADS = C_WIDTH // C_HEAD_DIM
POOL_WINDOWS = (2, 4, 8, 16)
POOL_GROUP = D_WIDTH // len(POOL_WINDOWS)
POOL_HIST = max(POOL_WINDOWS) - 1
D_IN = 2 * A_WIDTH + 3 * B_WIDTH + 2 * C_WIDTH + D_WIDTH
D_FF = 4 * D_MODEL
EPS = 1e-6

kernel_name = 'hybrid_streaming_encoder_step'


def rmsnorm(x, g):
    xf = x.astype(jnp.float32)
    y = xf * lax.rsqrt(jnp.mean(xf * xf, axis=-1, keepdims=True) + EPS)
    return (y * g.astype(jnp.float32)).astype(x.dtype)


def causal_dwconv(x, hist, w):
    k = w.shape[0]
    xp = jnp.concatenate([hist.astype(x.dtype), x], axis=1)
    y = lax.conv_general_dilated(xp, w[:, None, :].astype(x.dtype), window_strides=(1,), padding='VALID',
                                 dimension_numbers=('NWC', 'WIO', 'NWC'), feature_group_count=x.shape[-1])
    return y, xp[:, -(k - 1):]


def chunk_spatial_gate(u, v, ws, bias):
    b, t, _ = v.shape
    tp = -(-t // A_CHUNK) * A_CHUNK
    vp = jnp.pad(v, ((0, 0), (0, tp - t), (0, 0))).reshape(b, tp // A_CHUNK, A_CHUNK, A_HEADS, A_HEAD_DIM)
    wm = ws.astype(v.dtype) * jnp.tril(jnp.ones((A_CHUNK, A_CHUNK), v.dtype))
    mix = jnp.einsum('hij,bcjhd->bcihd', wm, vp) + bias.T.astype(v.dtype)[None, None, :, :, None]
    return u * mix.reshape(b, tp, A_WIDTH)[:, :t]


def multiscale_pool(p, hist, w_grp, scale, pos0):
    b, t, _ = p.shape
    xp = jnp.concatenate([hist.astype(p.dtype), p], axis=1)
    xf = xp.astype(jnp.float32)
    cs = jnp.concatenate([jnp.zeros((b, 1, D_WIDTH), jnp.float32), jnp.cumsum(xf, axis=1)], axis=1)
    pos = pos0 + jnp.arange(t)
    outs = []
    for g, w in enumerate(POOL_WINDOWS):
        sl = slice(g * POOL_GROUP, (g + 1) * POOL_GROUP)
        s = cs[:, POOL_HIST + 1:POOL_HIST + 1 + t, sl] - cs[:, POOL_HIST + 1 - w:POOL_HIST + 1 - w + t, sl]
        cnt = jnp.minimum(pos + 1, w).astype(jnp.float32)
        outs.append(s / cnt[None, :, None])
    pooled = jnp.concatenate(outs, axis=-1)
    d = (pooled - p.astype(jnp.float32)).astype(p.dtype).reshape(b, t, len(POOL_WINDOWS), POOL_GROUP)
    y = jnp.einsum('btgc,gcd->btgd', d, w_grp.astype(p.dtype)).reshape(b, t, D_WIDTH) * scale.astype(p.dtype)
    return y, xp[:, -POOL_HIST:]


def mixer_block(h, hist_b, hist_c, hist_d, pos0, w_in, a_ws, a_b, b_conv, c_conv, c_conv_b, c_ln_g, c_ln_b,
                d_w, d_scale, w_out):
    b, t, _ = h.shape
    z = h @ w_in.astype(h.dtype)
    a_u, a_v, b_h, b_b, b_c, c_a, c_g, d_p = jnp.split(z, 8, axis=-1)
    a_u = jax.nn.gelu(a_u, approximate=False)
    a_v = jax.nn.gelu(a_v, approximate=False)
    ya = chunk_spatial_gate(a_u, a_v, a_ws, a_b)
    conv_b, new_hb = causal_dwconv(b_c * b_h, hist_b, b_conv)
    yb = b_b * conv_b
    glu = c_a * jax.nn.sigmoid(c_g)
    conv_c, new_hc = causal_dwconv(glu, hist_c, c_conv)
    cf = (conv_c + c_conv_b.astype(h.dtype)).astype(jnp.float32).reshape(b, t, C_HEADS, C_HEAD_DIM)
    mu = jnp.mean(cf, axis=-1, keepdims=True)
    var = jnp.mean(jnp.square(cf - mu), axis=-1, keepdims=True)
    cn = ((cf - mu) * lax.rsqrt(var + EPS)).reshape(b, t, C_WIDTH)
    yc = jax.nn.silu(cn * c_ln_g.astype(jnp.float32) + c_ln_b.astype(jnp.float32)).astype(h.dtype)
    yd, new_hd = multiscale_pool(d_p, hist_d, d_w, d_scale, pos0)
    mix = jnp.concatenate([ya, yb, yc, yd], axis=-1) @ w_out.astype(h.dtype)
    return mix, new_hb, new_hc, new_hd, a_v


def trunk(x, hist_b, hist_c, hist_d, pos0, g_mix, w_in, a_ws, a_b, b_conv, c_conv, c_conv_b, c_ln_g, c_ln_b,
          d_w, d_scale, w_out, g_ffn, w_ff1, w_ff2, g_final):
    nb, nc, nd, av = [], [], [], []
    for l in range(DEPTH):
        h = rmsnorm(x, g_mix[l])
        m, hb, hc, hd, v = mixer_block(h, hist_b[l], hist_c[l], hist_d[l], pos0, w_in[l], a_ws[l], a_b[l],
                                       b_conv[l], c_conv[l], c_conv_b[l], c_ln_g[l], c_ln_b[l], d_w[l],
                                       d_scale[l], w_out[l])
        x = x + m
        h = rmsnorm(x, g_ffn[l])
        x = x + jnp.square(jax.nn.relu(h @ w_ff1[l].astype(h.dtype))) @ w_ff2[l].astype(h.dtype)
        nb.append(hb)
        nc.append(hc)
        nd.append(hd)
        av.append(v)
    return rmsnorm(x, g_final), jnp.stack(nb), jnp.stack(nc), jnp.stack(nd), jnp.stack(av)


def setup_inputs(seed: int = 0) -> dict:
    key = jax.random.key(seed)
    ks = jax.random.split(key, 24)

    def nrm(k, shape, s=1.0):
        return jax.random.normal(k, shape, jnp.float32) * s

    return {
        'x_prompt': nrm(ks[0], (BATCH, SEQ, D_MODEL)),
        'x_sample': nrm(ks[1], (DEC_BATCH, DEC_SEQ, D_MODEL)),
        'state_conv_b': nrm(ks[2], (DEPTH, DEC_BATCH, B_KERNEL - 1, B_WIDTH)),
        'state_conv_c': nrm(ks[3], (DEPTH, DEC_BATCH, C_KERNEL - 1, C_WIDTH), 0.5),
        'state_pool': nrm(ks[4], (DEPTH, DEC_BATCH, POOL_HIST, D_WIDTH)),
        'g_mix': 1.0 + nrm(ks[5], (DEPTH, D_MODEL), 0.1),
        'w_in': nrm(ks[6], (DEPTH, D_MODEL, D_IN), D_MODEL ** -0.5),
        'a_ws': nrm(ks[7], (DEPTH, A_HEADS, A_CHUNK, A_CHUNK), A_CHUNK ** -0.5),
        'a_b': 1.0 + nrm(ks[8], (DEPTH, A_HEADS, A_CHUNK), 0.1),
        'b_conv': nrm(ks[9], (DEPTH, B_KERNEL, B_WIDTH), B_KERNEL ** -0.5),
        'c_conv': nrm(ks[10], (DEPTH, C_KERNEL, C_WIDTH), C_KERNEL ** -0.5),
        'c_conv_b': nrm(ks[11], (DEPTH, C_WIDTH), 0.02),
        'c_ln_g': 1.0 + nrm(ks[12], (DEPTH, C_WIDTH), 0.1),
        'c_ln_b': nrm(ks[13], (DEPTH, C_WIDTH), 0.02),
        'd_w': nrm(ks[14], (DEPTH, len(POOL_WINDOWS), POOL_GROUP, POOL_GROUP), POOL_GROUP ** -0.5),
        'd_scale': 1.0 + nrm(ks[15], (DEPTH, D_WIDTH), 0.1),
        'w_out': nrm(ks[16], (DEPTH, D_MIX, D_MODEL), D_MIX ** -0.5),
        'g_ffn': 1.0 + nrm(ks[17], (DEPTH, D_MODEL), 0.1),
        'w_ff1': nrm(ks[18], (DEPTH, D_MODEL, D_FF), D_MODEL ** -0.5),
        'w_ff2': nrm(ks[19], (DEPTH, D_FF, D_MODEL), D_FF ** -0.5),
        'g_final': 1.0 + nrm(ks[20], (D_MODEL,), 0.1),
    }


def reference(x_prompt, x_sample, state_conv_b, state_conv_c, state_pool, g_mix, w_in, a_ws, a_b, b_conv,
              c_conv, c_conv_b, c_ln_g, c_ln_b, d_w, d_scale, w_out, g_ffn, w_ff1, w_ff2, g_final):
    bp = x_prompt.shape[0]
    zb = jnp.zeros((DEPTH, bp, B_KERNEL - 1, B_WIDTH), x_prompt.dtype)
    zc = jnp.zeros((DEPTH, bp, C_KERNEL - 1, C_WIDTH), x_prompt.dtype)
    zd = jnp.zeros((DEPTH, bp, POOL_HIST, D_WIDTH), x_prompt.dtype)
    y_prompt, new_conv_b_prompt, new_conv_c_prompt, new_pool_prompt, _ = trunk(
        x_prompt, zb, zc, zd, 0, g_mix, w_in, a_ws, a_b, b_conv, c_conv, c_conv_b, c_ln_g, c_ln_b,
        d_w, d_scale, w_out, g_ffn, w_ff1, w_ff2, g_final)
    y_sample, new_conv_b_sample, new_conv_c_sample, new_pool_sample, new_a_v_sample = trunk(
        x_sample, state_conv_b, state_conv_c, state_pool, PAST_LEN, g_mix, w_in, a_ws, a_b, b_conv, c_conv,
        c_conv_b, c_ln_g, c_ln_b, d_w, d_scale, w_out, g_ffn, w_ff1, w_ff2, g_final)
    return (y_prompt, y_sample, new_conv_b_prompt, new_conv_c_prompt, new_pool_prompt,
            new_conv_b_sample, new_conv_c_sample, new_pool_sample, new_a_v_sample)
```

```python
import functools
import math

import jax
import jax.numpy as jnp
from jax import lax
from jax.experimental import pallas as pl
from jax.experimental.pallas import tpu as pltpu

D_MODEL = 4096
GROUP_WIDTH = D_MODEL // 4
D_IN = 8 * GROUP_WIDTH
D_FF = 4 * D_MODEL
A_CHUNK = 128
HEAD_DIM = 128
N_HEADS = GROUP_WIDTH // HEAD_DIM
B_KERNEL = 3
C_KERNEL = 31
POOL_WINDOWS = (2, 4, 8, 16)
POOL_GROUP = GROUP_WIDTH // len(POOL_WINDOWS)
POOL_HIST = max(POOL_WINDOWS) - 1
PAST_LEN = 2048
EPS = 1e-6

SUBLANES = 8
VMEM_LIMIT_BYTES = 56 * 1024 * 1024

B_PAD = SUBLANES
C_PAD = 4 * SUBLANES
D_PAD = 2 * SUBLANES


def _cparams(semantics):
    return pltpu.CompilerParams(dimension_semantics=semantics, vmem_limit_bytes=VMEM_LIMIT_BYTES)


def _rmsnorm_kernel(x_ref, g_ref, o_ref):
    x = x_ref[...]
    ms = jnp.mean(x * x, axis=-1, keepdims=True)
    o_ref[...] = (x * lax.rsqrt(ms + EPS) * g_ref[...]).astype(o_ref.dtype)


def _rmsnorm(x, g, out_dtype, bm):
    m, d = x.shape
    return pl.pallas_call(
        _rmsnorm_kernel,
        grid=(m // bm,),
        in_specs=[pl.BlockSpec((bm, d), lambda i: (i, 0)), pl.BlockSpec((1, d), lambda i: (0, 0))],
        out_specs=pl.BlockSpec((bm, d), lambda i: (i, 0)),
        out_shape=jax.ShapeDtypeStruct((m, d), out_dtype),
        compiler_params=_cparams(("parallel",)),
        name="rmsnorm",
    )(x, g.reshape(1, d))


def _matmul_kernel(*refs, nk, relu2, has_res):
    x_ref, w_ref = refs[0], refs[1]
    res_ref = refs[2] if has_res else None
    o_ref = refs[2 + has_res]
    acc_ref = refs[3 + has_res] if nk > 1 else None

    def finish(acc):
        if relu2:
            r = jnp.maximum(acc, 0.0)
            acc = r * r
        if has_res:
            acc = res_ref[...] + acc
        o_ref[...] = acc.astype(o_ref.dtype)

    part = jnp.dot(x_ref[...], w_ref[...], preferred_element_type=jnp.float32)
    if nk == 1:
        finish(part)
    else:
        k = pl.program_id(2)

        @pl.when(k == 0)
        def _():
            acc_ref[...] = part

        @pl.when(k > 0)
        def _():
            acc_ref[...] += part

        @pl.when(k == nk - 1)
        def _():
            finish(acc_ref[...])


def _matmul(x, w, *, bm, bn, bk, out_dtype, relu2=False, res=None, name):
    m, kdim = x.shape
    n = w.shape[1]
    nk = kdim // bk
    in_specs = [pl.BlockSpec((bm, bk), lambda i, j, k: (i, k)), pl.BlockSpec((bk, bn), lambda i, j, k: (k, j))]
    args = [x, w]
    if res is not None:
        in_specs.append(pl.BlockSpec((bm, bn), lambda i, j, k: (i, j)))
        args.append(res)
    scratch = [pltpu.VMEM((bm, bn), jnp.float32)] if nk > 1 else []
    return pl.pallas_call(
        functools.partial(_matmul_kernel, nk=nk, relu2=relu2, has_res=res is not None),
        grid=(m // bm, n // bn, nk),
        in_specs=in_specs,
        out_specs=pl.BlockSpec((bm, bn), lambda i, j, k: (i, j)),
        out_shape=jax.ShapeDtypeStruct((m, n), out_dtype),
        scratch_shapes=scratch,
        compiler_params=_cparams(("parallel", "parallel", "arbitrary")),
        name=name,
    )(*args)


def _gelu(x):
    return 0.5 * x * (1.0 + lax.erf(x * math.sqrt(0.5)))


def _sigmoid(x):
    return 1.0 / (1.0 + jnp.exp(-x))


def _mixer_kernel(z_ref, sb_ref, sc_ref, sd_ref, ws_ref, ab_ref, bconv_ref, cconv_ref, ccb_ref, lng_ref, lnb_ref,
                  dw_ref, dscale_ref, mix_ref, nb_ref, nc_ref, nd_ref, av_ref, bufb, bufc, bufd, *, tm, pos0):
    t = pl.program_id(1)
    gw = GROUP_WIDTH

    @pl.when(t == 0)
    def _():
        bufb[B_PAD - (B_KERNEL - 1):B_PAD, :] = sb_ref[0]
        bufc[C_PAD - (C_KERNEL - 1):C_PAD, :] = sc_ref[0]
        bufd[D_PAD - POOL_HIST:D_PAD, :] = sd_ref[0]

    bufb[B_PAD:B_PAD + tm, :] = z_ref[0, :, 4 * gw:5 * gw] * z_ref[0, :, 2 * gw:3 * gw]
    bufc[C_PAD:C_PAD + tm, :] = z_ref[0, :, 5 * gw:6 * gw] * _sigmoid(z_ref[0, :, 6 * gw:7 * gw])
    bufd[D_PAD:D_PAD + tm, :] = z_ref[0, :, 7 * gw:8 * gw]

    nch = max(tm // A_CHUNK, 1)
    cr = min(tm, A_CHUNK)
    tri = lax.broadcasted_iota(jnp.int32, (cr, cr), 0) >= lax.broadcasted_iota(jnp.int32, (cr, cr), 1)

    for h in range(N_HEADS):
        cs = slice(h * HEAD_DIM, (h + 1) * HEAD_DIM)

        wm = jnp.where(tri, ws_ref[h, 0:cr, 0:cr], 0.0).astype(jnp.bfloat16)
        bias = ab_ref[h, 0:cr, :]
        for c in range(nch):
            rs = slice(c * cr, (c + 1) * cr)
            u = _gelu(z_ref[0, rs, h * HEAD_DIM:(h + 1) * HEAD_DIM])
            v = _gelu(z_ref[0, rs, gw + h * HEAD_DIM:gw + (h + 1) * HEAD_DIM])
            if av_ref is not None:
                av_ref[0, rs, cs] = v
            mixv = jnp.dot(wm, v.astype(jnp.bfloat16), preferred_element_type=jnp.float32) + bias
            mix_ref[0, rs, cs] = (u * mixv).astype(mix_ref.dtype)

        conv_b = bconv_ref[0:1, cs] * bufb[B_PAD - 2:B_PAD - 2 + tm, cs]
        for k in range(1, B_KERNEL):
            conv_b = conv_b + bconv_ref[k:k + 1, cs] * bufb[B_PAD - 2 + k:B_PAD - 2 + k + tm, cs]
        yb = z_ref[0, :, 3 * gw + h * HEAD_DIM:3 * gw + (h + 1) * HEAD_DIM] * conv_b
        mix_ref[0, :, gw + h * HEAD_DIM:gw + (h + 1) * HEAD_DIM] = yb.astype(mix_ref.dtype)

        off = C_PAD - (C_KERNEL - 1)
        conv_c = cconv_ref[0:1, cs] * bufc[off:off + tm, cs]
        for k in range(1, C_KERNEL):
            conv_c = conv_c + cconv_ref[k:k + 1, cs] * bufc[off + k:off + k + tm, cs]
        cf = conv_c + ccb_ref[0:1, cs]
        mu = jnp.mean(cf, axis=-1, keepdims=True)
        cen = cf - mu
        var = jnp.mean(cen * cen, axis=-1, keepdims=True)
        cn = cen * lax.rsqrt(var + EPS)
        a = cn * lng_ref[0:1, cs] + lnb_ref[0:1, cs]
        yc = a * _sigmoid(a)
        mix_ref[0, :, 2 * gw + h * HEAD_DIM:2 * gw + (h + 1) * HEAD_DIM] = yc.astype(mix_ref.dtype)

    pos = pos0 + t * tm + lax.broadcasted_iota(jnp.int32, (tm, POOL_GROUP), 0)
    for g, w in enumerate(POOL_WINDOWS):
        gs = slice(g * POOL_GROUP, (g + 1) * POOL_GROUP)
        p = bufd[D_PAD:D_PAD + tm, gs]
        s = p
        for i in range(1, w):
            s = s + bufd[D_PAD - i:D_PAD - i + tm, gs]
        cnt = jnp.minimum(pos + 1, w).astype(jnp.float32)
        d = (s / cnt - p).astype(jnp.bfloat16)
        yd = jnp.dot(d, dw_ref[g].astype(jnp.bfloat16), preferred_element_type=jnp.float32) * dscale_ref[0:1, gs]
        mix_ref[0, :, 3 * gw + g * POOL_GROUP:3 * gw + (g + 1) * POOL_GROUP] = yd.astype(mix_ref.dtype)

    nb = bufb[tm + B_PAD - (B_KERNEL - 1):tm + B_PAD, :]
    nc = bufc[tm + C_PAD - (C_KERNEL - 1):tm + C_PAD, :]
    nd = bufd[tm + D_PAD - POOL_HIST:tm + D_PAD, :]
    nb_ref[0] = nb
    nc_ref[0] = nc
    nd_ref[0] = nd
    bufb[B_PAD - (B_KERNEL - 1):B_PAD, :] = nb
    bufc[C_PAD - (C_KERNEL - 1):C_PAD, :] = nc
    bufd[D_PAD - POOL_HIST:D_PAD, :] = nd


def _mixer_kernel_no_av(*refs, tm, pos0):
    ins, outs, scratch = refs[:13], refs[13:17], refs[17:]
    _mixer_kernel(*ins, *outs, None, *scratch, tm=tm, pos0=pos0)


def _mixers(z, sb, sc, sd, ws, ab, bconv, cconv, ccb, lng, lnb, dw, dscale, *, tm, pos0, want_av):
    b, t, _ = z.shape
    gw = GROUP_WIDTH
    seq = lambda i, j: (i, 0, 0)
    fixed2 = lambda i, j: (0, 0)
    fixed3 = lambda i, j: (0, 0, 0)
    in_specs = [
        pl.BlockSpec((1, tm, D_IN), lambda i, j: (i, j, 0)),
        pl.BlockSpec((1, B_KERNEL - 1, gw), seq),
        pl.BlockSpec((1, C_KERNEL - 1, gw), seq),
        pl.BlockSpec((1, POOL_HIST, gw), seq),
        pl.BlockSpec((N_HEADS, A_CHUNK, A_CHUNK), fixed3),
        pl.BlockSpec((N_HEADS, A_CHUNK, HEAD_DIM), fixed3),
        pl.BlockSpec((B_KERNEL, gw), fixed2),
        pl.BlockSpec((C_KERNEL, gw), fixed2),
        pl.BlockSpec((1, gw), fixed2),
        pl.BlockSpec((1, gw), fixed2),
        pl.BlockSpec((1, gw), fixed2),
        pl.BlockSpec((len(POOL_WINDOWS), POOL_GROUP, POOL_GROUP), fixed3),
        pl.BlockSpec((1, gw), fixed2),
    ]
    out_specs = [
        pl.BlockSpec((1, tm, D_MODEL), lambda i, j: (i, j, 0)),
        pl.BlockSpec((1, B_KERNEL - 1, gw), seq),
        pl.BlockSpec((1, C_KERNEL - 1, gw), seq),
        pl.BlockSpec((1, POOL_HIST, gw), seq),
    ]
    out_shape = [
        jax.ShapeDtypeStruct((b, t, D_MODEL), jnp.bfloat16),
        jax.ShapeDtypeStruct((b, B_KERNEL - 1, gw), jnp.float32),
        jax.ShapeDtypeStruct((b, C_KERNEL - 1, gw), jnp.float32),
        jax.ShapeDtypeStruct((b, POOL_HIST, gw), jnp.float32),
    ]
    if want_av:
        out_specs.append(pl.BlockSpec((1, tm, gw), lambda i, j: (i, j, 0)))
        out_shape.append(jax.ShapeDtypeStruct((b, t, gw), jnp.float32))
        body = functools.partial(_mixer_kernel, tm=tm, pos0=pos0)
    else:
        body = functools.partial(_mixer_kernel_no_av, tm=tm, pos0=pos0)
    ab_rows = jnp.broadcast_to(ab[:, :, None], (N_HEADS, A_CHUNK, HEAD_DIM))
    return pl.pallas_call(
        body,
        grid=(b, t // tm),
        in_specs=in_specs,
        out_specs=out_specs,
        out_shape=out_shape,
        scratch_shapes=[
            pltpu.VMEM((B_PAD + tm, gw), jnp.float32),
            pltpu.VMEM((C_PAD + tm, gw), jnp.float32),
            pltpu.VMEM((D_PAD + tm, gw), jnp.float32),
        ],
        compiler_params=_cparams(("parallel", "arbitrary")),
        name="mixers",
    )(z, sb, sc, sd, ws, ab_rows, bconv, cconv, ccb.reshape(1, gw), lng.reshape(1, gw), lnb.reshape(1, gw),
      dw, dscale.reshape(1, gw))


def _trunk(x, hist_b, hist_c, hist_d, pos0, want_av, weights, *, bm, tm):
    (g_mix, w_in, a_ws, a_b, b_conv, c_conv, c_conv_b, c_ln_g, c_ln_b, d_w, d_scale, w_out, g_ffn, w_ff1, w_ff2,
     g_final) = weights
    b, t, d = x.shape
    m = b * t
    depth = g_mix.shape[0]
    xf = x.reshape(m, d)
    nbs, ncs, nds, avs = [], [], [], []
    for l in range(depth):
        h = _rmsnorm(xf, g_mix[l], jnp.bfloat16, bm // 2)
        z = _matmul(h, w_in[l], bm=bm, bn=1024, bk=D_MODEL, out_dtype=jnp.float32, name="w_in")
        outs = _mixers(z.reshape(b, t, D_IN), hist_b[l], hist_c[l], hist_d[l], a_ws[l], a_b[l], b_conv[l], c_conv[l],
                       c_conv_b[l], c_ln_g[l], c_ln_b[l], d_w[l], d_scale[l], tm=tm, pos0=pos0, want_av=want_av)
        mix = outs[0].reshape(m, D_MODEL)
        nbs.append(outs[1])
        ncs.append(outs[2])
        nds.append(outs[3])
        if want_av:
            avs.append(outs[4])
        xf = _matmul(mix, w_out[l], bm=bm, bn=1024, bk=D_MODEL, out_dtype=jnp.float32, res=xf, name="w_out")
        h = _rmsnorm(xf, g_ffn[l], jnp.bfloat16, bm // 2)
        u = _matmul(h, w_ff1[l], bm=bm, bn=1024, bk=D_MODEL, out_dtype=jnp.bfloat16, relu2=True, name="w_ff1")
        xf = _matmul(u, w_ff2[l], bm=bm, bn=1024, bk=2048, out_dtype=jnp.float32, res=xf, name="w_ff2")
    y = _rmsnorm(xf, g_final, jnp.float32, bm // 2).reshape(b, t, d)
    return y, jnp.stack(nbs), jnp.stack(ncs), jnp.stack(nds), (jnp.stack(avs) if want_av else None)


def kernel(x_prompt, x_sample, state_conv_b, state_conv_c, state_pool, g_mix, w_in, a_ws, a_b, b_conv, c_conv,
           c_conv_b, c_ln_g, c_ln_b, d_w, d_scale, w_out, g_ffn, w_ff1, w_ff2, g_final):
    depth = g_mix.shape[0]
    bp = x_prompt.shape[0]
    bf = jnp.bfloat16
    weights = (g_mix, w_in.astype(bf), a_ws, a_b, b_conv, c_conv, c_conv_b, c_ln_g, c_ln_b, d_w, d_scale,
               w_out.astype(bf), g_ffn, w_ff1.astype(bf), w_ff2.astype(bf), g_final)
    zb = jnp.zeros((depth, bp, B_KERNEL - 1, GROUP_WIDTH), x_prompt.dtype)
    zc = jnp.zeros((depth, bp, C_KERNEL - 1, GROUP_WIDTH), x_prompt.dtype)
    zd = jnp.zeros((depth, bp, POOL_HIST, GROUP_WIDTH), x_prompt.dtype)
    y_p, nb_p, nc_p, nd_p, _ = _trunk(x_prompt, zb, zc, zd, 0, False, weights, bm=1024, tm=A_CHUNK)
    y_s, nb_s, nc_s, nd_s, av_s = _trunk(x_sample, state_conv_b, state_conv_c, state_pool, PAST_LEN, True, weights,
                                         bm=512, tm=x_sample.shape[1])
    return (y_p, y_s, nb_p, nc_p, nd_p, nb_s, nc_s, nd_s, av_s)
```

```python
import functools
import math

import jax
import jax.numpy as jnp
from jax import lax
from jax.experimental import pallas as pl
from jax.experimental.pallas import tpu as pltpu

D_MODEL = 4096
GROUP_WIDTH = D_MODEL // 4
D_IN = 8 * GROUP_WIDTH
D_FF = 4 * D_MODEL
A_CHUNK = 128
HEAD_DIM = 128
N_HEADS = GROUP_WIDTH // HEAD_DIM
B_KERNEL = 3
C_KERNEL = 31
POOL_WINDOWS = (2, 4, 8, 16)
POOL_GROUP = GROUP_WIDTH // len(POOL_WINDOWS)
POOL_HIST = max(POOL_WINDOWS) - 1
PAST_LEN = 2048
EPS = 1e-6

SUBLANES = 8
LANES = 128
VMEM_LIMIT_BYTES = 60 * 1024 * 1024

ROW_STRIDE = 2
B_PAD = SUBLANES
C_PAD = 4 * SUBLANES
D_PAD = 2 * SUBLANES


def _cparams(semantics):
    return pltpu.CompilerParams(dimension_semantics=semantics, vmem_limit_bytes=VMEM_LIMIT_BYTES)


def _rmsnorm_kernel(x_ref, g_ref, o_ref):
    x = x_ref[...]
    ms = jnp.mean(x * x, axis=-1, keepdims=True)
    o_ref[...] = (x * lax.rsqrt(ms + EPS) * g_ref[...]).astype(o_ref.dtype)


def _rmsnorm(x, g, out_dtype, bm):
    m, d = x.shape
    return pl.pallas_call(
        _rmsnorm_kernel,
        grid=(m // bm,),
        in_specs=[pl.BlockSpec((bm, d), lambda i: (i, 0)), pl.BlockSpec((1, d), lambda i: (0, 0))],
        out_specs=pl.BlockSpec((bm, d), lambda i: (i, 0)),
        out_shape=jax.ShapeDtypeStruct((m, d), out_dtype),
        compiler_params=_cparams(("parallel",)),
        name="rmsnorm",
    )(x, g.reshape(1, d))


def _matmul_kernel(*refs, nk, relu2, has_res):
    x_ref, w_ref = refs[0], refs[1]
    res_ref = refs[2] if has_res else None
    o_ref = refs[2 + has_res]

    part = jnp.dot(x_ref[...], w_ref[...], preferred_element_type=jnp.float32)
    if nk == 1:
        if relu2:
            r = jnp.maximum(part, 0.0)
            part = r * r
        if has_res:
            part = res_ref[...] + part
        o_ref[...] = part.astype(o_ref.dtype)
    else:
        assert not relu2 and o_ref.dtype == jnp.float32
        k = pl.program_id(2)

        @pl.when(k == 0)
        def _():
            o_ref[...] = (res_ref[...] + part) if has_res else part

        @pl.when(k > 0)
        def _():
            o_ref[...] += part


def _matmul(x, w, layer, *, bm, bn, bk, out_dtype, relu2=False, res=None, name):
    m, kdim = x.shape
    n = w.shape[2]
    nk = kdim // bk
    in_specs = [pl.BlockSpec((bm, bk), lambda i, j, k: (i, k)),
                pl.BlockSpec((None, bk, bn), lambda i, j, k: (layer, k, j))]
    args = [x, w]
    if res is not None:
        in_specs.append(pl.BlockSpec((bm, bn), lambda i, j, k: (i, j)))
        args.append(res)
    return pl.pallas_call(
        functools.partial(_matmul_kernel, nk=nk, relu2=relu2, has_res=res is not None),
        grid=(m // bm, n // bn, nk),
        in_specs=in_specs,
        out_specs=pl.BlockSpec((bm, bn), lambda i, j, k: (i, j)),
        out_shape=jax.ShapeDtypeStruct((m, n), out_dtype),
        compiler_params=_cparams(("parallel", "parallel", "arbitrary")),
        name=name,
    )(*args)


def _gelu(x):
    return 0.5 * x * (1.0 + lax.erf(x * math.sqrt(0.5)))


def _sigmoid(x):
    return 1.0 / (1.0 + jnp.exp(-x))


def _rows(first, count):
    return pl.ds(ROW_STRIDE * first, count, stride=ROW_STRIDE)


def _mixer_kernel(z_ref, sb_ref, sc_ref, sd_ref, ws_ref, ab_ref, bconv_ref, cconv_ref, ccb_ref, lng_ref, lnb_ref,
                  dw_ref, dscale_ref, mix_ref, nb_ref, nc_ref, nd_ref, av_ref, bufb, bufc, bufd, *, tm, pos0):
    t = pl.program_id(1)
    gw = GROUP_WIDTH
    hb, hc, hd = B_KERNEL - 1, C_KERNEL - 1, POOL_HIST

    def zcols(group, h):
        return z_ref[0, :, group * gw + h * HEAD_DIM:group * gw + (h + 1) * HEAD_DIM]

    @pl.when(t == 0)
    def _():
        for h in range(N_HEADS):
            cs = slice(h * HEAD_DIM, (h + 1) * HEAD_DIM)
            bufb[h, _rows(B_PAD - hb, hb), :] = sb_ref[0, :, cs]
            bufc[h, _rows(C_PAD - hc, hc), :] = sc_ref[0, :, cs]
            bufd[h, _rows(D_PAD - hd, hd), :] = sd_ref[0, :, cs]

    nch = max(tm // A_CHUNK, 1)
    cr = min(tm, A_CHUNK)
    tri = lax.broadcasted_iota(jnp.int32, (cr, cr), 0) >= lax.broadcasted_iota(jnp.int32, (cr, cr), 1)
    pos = pos0 + t * tm + lax.broadcasted_iota(jnp.int32, (tm, HEAD_DIM), 0)

    pooled_minus_token = []
    for h in range(N_HEADS):
        cs = slice(h * HEAD_DIM, (h + 1) * HEAD_DIM)

        wm = jnp.where(tri, ws_ref[h, 0:cr, 0:cr], 0.0).astype(jnp.bfloat16)
        bias = ab_ref[h, 0:cr, :]
        for c in range(nch):
            rs = slice(c * cr, (c + 1) * cr)
            u = _gelu(z_ref[0, rs, h * HEAD_DIM:(h + 1) * HEAD_DIM])
            v = _gelu(z_ref[0, rs, gw + h * HEAD_DIM:gw + (h + 1) * HEAD_DIM])
            if av_ref is not None:
                av_ref[0, rs, cs] = v
            mixv = jnp.dot(wm, v.astype(jnp.bfloat16), preferred_element_type=jnp.float32) + bias
            mix_ref[0, rs, cs] = (u * mixv).astype(mix_ref.dtype)

        bufb[h, _rows(B_PAD, tm), :] = zcols(4, h) * zcols(2, h)
        conv_b = bconv_ref[0:1, cs] * bufb[h, _rows(B_PAD - hb, tm), :]
        for k in range(1, B_KERNEL):
            conv_b = conv_b + bconv_ref[k:k + 1, cs] * bufb[h, _rows(B_PAD - hb + k, tm), :]
        mix_ref[0, :, gw + h * HEAD_DIM:gw + (h + 1) * HEAD_DIM] = (zcols(3, h) * conv_b).astype(mix_ref.dtype)

        bufc[h, _rows(C_PAD, tm), :] = zcols(5, h) * _sigmoid(zcols(6, h))
        conv_c = cconv_ref[0:1, cs] * bufc[h, _rows(C_PAD - hc, tm), :]
        for k in range(1, C_KERNEL):
            conv_c = conv_c + cconv_ref[k:k + 1, cs] * bufc[h, _rows(C_PAD - hc + k, tm), :]
        cf = conv_c + ccb_ref[0:1, cs]
        mu = jnp.mean(cf, axis=-1, keepdims=True)
        cen = cf - mu
        var = jnp.mean(cen * cen, axis=-1, keepdims=True)
        a = cen * lax.rsqrt(var + EPS) * lng_ref[0:1, cs] + lnb_ref[0:1, cs]
        mix_ref[0, :, 2 * gw + h * HEAD_DIM:2 * gw + (h + 1) * HEAD_DIM] = (a * _sigmoid(a)).astype(mix_ref.dtype)

        w = POOL_WINDOWS[h * HEAD_DIM // POOL_GROUP]
        p = zcols(7, h)
        bufd[h, _rows(D_PAD, tm), :] = p
        s = p
        for i in range(1, w):
            s = s + bufd[h, _rows(D_PAD - i, tm), :]
        cnt = jnp.minimum(pos + 1, w).astype(jnp.float32)
        pooled_minus_token.append((s / cnt - p).astype(jnp.bfloat16))

        nb = bufb[h, _rows(tm + B_PAD - hb, hb), :]
        nc = bufc[h, _rows(tm + C_PAD - hc, hc), :]
        nd = bufd[h, _rows(tm + D_PAD - hd, hd), :]
        nb_ref[0, :, cs] = nb
        nc_ref[0, :, cs] = nc
        nd_ref[0, :, cs] = nd
        bufb[h, _rows(B_PAD - hb, hb), :] = nb
        bufc[h, _rows(C_PAD - hc, hc), :] = nc
        bufd[h, _rows(D_PAD - hd, hd), :] = nd

    slabs = POOL_GROUP // HEAD_DIM
    for g in range(len(POOL_WINDOWS)):
        gs = slice(g * POOL_GROUP, (g + 1) * POOL_GROUP)
        d = jnp.concatenate(pooled_minus_token[g * slabs:(g + 1) * slabs], axis=-1)
        yd = jnp.dot(d, dw_ref[g].astype(jnp.bfloat16), preferred_element_type=jnp.float32) * dscale_ref[0:1, gs]
        mix_ref[0, :, 3 * gw + g * POOL_GROUP:3 * gw + (g + 1) * POOL_GROUP] = yd.astype(mix_ref.dtype)


def _mixer_kernel_no_av(*refs, tm, pos0):
    ins, outs, scratch = refs[:13], refs[13:17], refs[17:]
    _mixer_kernel(*ins, *outs, None, *scratch, tm=tm, pos0=pos0)


def _mixers(z, sb, sc, sd, layer, ws, ab_rows, bconv, cconv, ccb, lng, lnb, dw, dscale, *, tm, pos0, want_av):
    b, t, _ = z.shape
    gw = GROUP_WIDTH
    seq = lambda i, j: (layer, i, 0, 0)
    par3 = lambda i, j: (layer, 0, 0)
    par4 = lambda i, j: (layer, 0, 0, 0)
    in_specs = [
        pl.BlockSpec((1, tm, D_IN), lambda i, j: (i, j, 0)),
        pl.BlockSpec((None, 1, B_KERNEL - 1, gw), seq),
        pl.BlockSpec((None, 1, C_KERNEL - 1, gw), seq),
        pl.BlockSpec((None, 1, POOL_HIST, gw), seq),
        pl.BlockSpec((None, N_HEADS, A_CHUNK, A_CHUNK), par4),
        pl.BlockSpec((None, N_HEADS, A_CHUNK, HEAD_DIM), par4),
        pl.BlockSpec((None, B_KERNEL, gw), par3),
        pl.BlockSpec((None, C_KERNEL, gw), par3),
        pl.BlockSpec((None, 1, gw), par3),
        pl.BlockSpec((None, 1, gw), par3),
        pl.BlockSpec((None, 1, gw), par3),
        pl.BlockSpec((None, len(POOL_WINDOWS), POOL_GROUP, POOL_GROUP), par4),
        pl.BlockSpec((None, 1, gw), par3),
    ]
    hist = lambda i, j: (i, 0, 0)
    out_specs = [
        pl.BlockSpec((1, tm, D_MODEL), lambda i, j: (i, j, 0)),
        pl.BlockSpec((1, B_KERNEL - 1, gw), hist),
        pl.BlockSpec((1, C_KERNEL - 1, gw), hist),
        pl.BlockSpec((1, POOL_HIST, gw), hist),
    ]
    out_shape = [
        jax.ShapeDtypeStruct((b, t, D_MODEL), jnp.bfloat16),
        jax.ShapeDtypeStruct((b, B_KERNEL - 1, gw), jnp.float32),
        jax.ShapeDtypeStruct((b, C_KERNEL - 1, gw), jnp.float32),
        jax.ShapeDtypeStruct((b, POOL_HIST, gw), jnp.float32),
    ]
    if want_av:
        out_specs.append(pl.BlockSpec((1, tm, gw), lambda i, j: (i, j, 0)))
        out_shape.append(jax.ShapeDtypeStruct((b, t, gw), jnp.float32))
        body = functools.partial(_mixer_kernel, tm=tm, pos0=pos0)
    else:
        body = functools.partial(_mixer_kernel_no_av, tm=tm, pos0=pos0)
    return pl.pallas_call(
        body,
        grid=(b, t // tm),
        in_specs=in_specs,
        out_specs=out_specs,
        out_shape=out_shape,
        scratch_shapes=[
            pltpu.VMEM((N_HEADS, ROW_STRIDE * (B_PAD + tm), LANES), jnp.float32),
            pltpu.VMEM((N_HEADS, ROW_STRIDE * (C_PAD + tm), LANES), jnp.float32),
            pltpu.VMEM((N_HEADS, ROW_STRIDE * (D_PAD + tm), LANES), jnp.float32),
        ],
        compiler_params=_cparams(("parallel", "arbitrary")),
        name="mixers",
    )(z, sb, sc, sd, ws, ab_rows, bconv, cconv, ccb, lng, lnb, dw, dscale)


def _trunk(x, hist_b, hist_c, hist_d, pos0, want_av, weights, *, bm, tm):
    (g_mix, w_in, a_ws, ab_rows, b_conv, c_conv, c_conv_b, c_ln_g, c_ln_b, d_w, d_scale, w_out, g_ffn, w_ff1, w_ff2,
     g_final) = weights
    b, t, d = x.shape
    m = b * t
    depth = g_mix.shape[0]
    xf = x.reshape(m, d)
    nbs, ncs, nds, avs = [], [], [], []
    for l in range(depth):
        h = _rmsnorm(xf, g_mix[l], jnp.bfloat16, bm // 2)
        z = _matmul(h, w_in, l, bm=bm, bn=1024, bk=D_MODEL, out_dtype=jnp.float32, name="w_in")
        outs = _mixers(z.reshape(b, t, D_IN), hist_b, hist_c, hist_d, l, a_ws, ab_rows, b_conv, c_conv, c_conv_b,
                       c_ln_g, c_ln_b, d_w, d_scale, tm=tm, pos0=pos0, want_av=want_av)
        mix = outs[0].reshape(m, D_MODEL)
        nbs.append(outs[1])
        ncs.append(outs[2])
        nds.append(outs[3])
        if want_av:
            avs.append(outs[4])
        xf = _matmul(mix, w_out, l, bm=bm, bn=1024, bk=D_MODEL, out_dtype=jnp.float32, res=xf, name="w_out")
        h = _rmsnorm(xf, g_ffn[l], jnp.bfloat16, bm // 2)
        u = _matmul(h, w_ff1, l, bm=bm, bn=1024, bk=D_MODEL, out_dtype=jnp.bfloat16, relu2=True, name="w_ff1")
        xf = _matmul(u, w_ff2, l, bm=bm, bn=1024, bk=D_MODEL, out_dtype=jnp.float32, res=xf, name="w_ff2")
    y = _rmsnorm(xf, g_final, jnp.float32, bm // 2).reshape(b, t, d)
    return y, jnp.stack(nbs), jnp.stack(ncs), jnp.stack(nds), (jnp.stack(avs) if want_av else None)


def kernel(x_prompt, x_sample, state_conv_b, state_conv_c, state_pool, g_mix, w_in, a_ws, a_b, b_conv, c_conv,
           c_conv_b, c_ln_g, c_ln_b, d_w, d_scale, w_out, g_ffn, w_ff1, w_ff2, g_final):
    depth = g_mix.shape[0]
    bp = x_prompt.shape[0]
    gw = GROUP_WIDTH
    bf = jnp.bfloat16
    ab_rows = jnp.broadcast_to(a_b[:, :, :, None], (depth, N_HEADS, A_CHUNK, HEAD_DIM))
    row = lambda p: p.reshape(depth, 1, gw)
    weights = (g_mix, w_in.astype(bf), a_ws, ab_rows, b_conv, c_conv, row(c_conv_b), row(c_ln_g), row(c_ln_b), d_w,
               row(d_scale), w_out.astype(bf), g_ffn, w_ff1.astype(bf), w_ff2.astype(bf), g_final)
    zb = jnp.zeros((depth, bp, B_KERNEL - 1, gw), x_prompt.dtype)
    zc = jnp.zeros((depth, bp, C_KERNEL - 1, gw), x_prompt.dtype)
    zd = jnp.zeros((depth, bp, POOL_HIST, gw), x_prompt.dtype)
    y_p, nb_p, nc_p, nd_p, _ = _trunk(x_prompt, zb, zc, zd, 0, False, weights, bm=1024, tm=A_CHUNK)
    y_s, nb_s, nc_s, nd_s, av_s = _trunk(x_sample, state_conv_b, state_conv_c, state_pool, PAST_LEN, True, weights,
                                         bm=512, tm=x_sample.shape[1])
    return (y_p, y_s, nb_p, nc_p, nd_p, nb_s, nc_s, nd_s, av_s)
```

```python
import functools
import math

import jax
import jax.numpy as jnp
from jax import lax
from jax.experimental import pallas as pl
from jax.experimental.pallas import tpu as pltpu

D_MODEL = 4096
GROUP_WIDTH = D_MODEL // 4
D_IN = 8 * GROUP_WIDTH
D_FF = 4 * D_MODEL
A_CHUNK = 128
HEAD_DIM = 128
N_HEADS = GROUP_WIDTH // HEAD_DIM
B_KERNEL = 3
C_KERNEL = 31
POOL_WINDOWS = (2, 4, 8, 16)
POOL_GROUP = GROUP_WIDTH // len(POOL_WINDOWS)
POOL_HIST = max(POOL_WINDOWS) - 1
PAST_LEN = 2048
EPS = 1e-6

SUBLANES = 8
LANES = 128
VMEM_LIMIT_BYTES = 60 * 1024 * 1024

ROW_STRIDE = 2
B_PAD = SUBLANES
C_PAD = 4 * SUBLANES
D_PAD = 2 * SUBLANES


def _cparams(semantics):
    return pltpu.CompilerParams(dimension_semantics=semantics, vmem_limit_bytes=VMEM_LIMIT_BYTES)


def _rmsnorm_kernel(x_ref, g_ref, o_ref):
    x = x_ref[...]
    ms = jnp.mean(x * x, axis=-1, keepdims=True)
    o_ref[...] = (x * lax.rsqrt(ms + EPS) * g_ref[...]).astype(o_ref.dtype)


def _rmsnorm(x, g, out_dtype, bm):
    m, d = x.shape
    return pl.pallas_call(
        _rmsnorm_kernel,
        grid=(m // bm,),
        in_specs=[pl.BlockSpec((bm, d), lambda i: (i, 0)), pl.BlockSpec((1, d), lambda i: (0, 0))],
        out_specs=pl.BlockSpec((bm, d), lambda i: (i, 0)),
        out_shape=jax.ShapeDtypeStruct((m, d), out_dtype),
        compiler_params=_cparams(("parallel",)),
        name="rmsnorm",
    )(x, g.reshape(1, d))


def _matmul_kernel(*refs, nk, relu2, has_res, emit_bf16):
    x_ref, w_ref = refs[0], refs[1]
    res_ref = refs[2] if has_res else None
    o_ref = refs[2 + has_res]

    w = w_ref[...]
    if emit_bf16:
        w = w.astype(jnp.bfloat16)
        refs[3 + has_res][...] = w
    part = jnp.dot(x_ref[...], w, preferred_element_type=jnp.float32)
    if nk == 1:
        if relu2:
            r = jnp.maximum(part, 0.0)
            part = r * r
        if has_res:
            part = res_ref[...] + part
        o_ref[...] = part.astype(o_ref.dtype)
    else:
        assert not relu2 and o_ref.dtype == jnp.float32
        k = pl.program_id(2)

        @pl.when(k == 0)
        def _():
            o_ref[...] = (res_ref[...] + part) if has_res else part

        @pl.when(k > 0)
        def _():
            o_ref[...] += part


def _matmul(x, w, layer=None, *, bm, bn, bk, out_dtype, relu2=False, res=None, name):
    m, kdim = x.shape
    emit_bf16 = layer is not None
    n = w.shape[-1]
    nk = kdim // bk
    if emit_bf16:
        assert m == bm and w.dtype == jnp.float32
        w_spec = pl.BlockSpec((None, bk, bn), lambda i, j, k: (layer, k, j))
    else:
        assert w.dtype == jnp.bfloat16
        w_spec = pl.BlockSpec((bk, bn), lambda i, j, k: (k, j))
    in_specs = [pl.BlockSpec((bm, bk), lambda i, j, k: (i, k)), w_spec]
    args = [x, w]
    if res is not None:
        in_specs.append(pl.BlockSpec((bm, bn), lambda i, j, k: (i, j)))
        args.append(res)
    out_specs = [pl.BlockSpec((bm, bn), lambda i, j, k: (i, j))]
    out_shape = [jax.ShapeDtypeStruct((m, n), out_dtype)]
    if emit_bf16:
        out_specs.append(pl.BlockSpec((bk, bn), lambda i, j, k: (k, j)))
        out_shape.append(jax.ShapeDtypeStruct((kdim, n), jnp.bfloat16))
    outs = pl.pallas_call(
        functools.partial(_matmul_kernel, nk=nk, relu2=relu2, has_res=res is not None, emit_bf16=emit_bf16),
        grid=(m // bm, n // bn, nk),
        in_specs=in_specs,
        out_specs=out_specs,
        out_shape=out_shape,
        compiler_params=_cparams(("parallel", "parallel", "arbitrary")),
        name=name,
    )(*args)
    return outs if emit_bf16 else outs[0]


def _gelu(x):
    return 0.5 * x * (1.0 + lax.erf(x * math.sqrt(0.5)))


def _sigmoid(x):
    return 1.0 / (1.0 + jnp.exp(-x))


def _rows(first, count):
    return pl.ds(ROW_STRIDE * first, count, stride=ROW_STRIDE)


def _mixer_kernel(z_ref, sb_ref, sc_ref, sd_ref, ws_ref, ab_ref, bconv_ref, cconv_ref, ccb_ref, lng_ref, lnb_ref,
                  dw_ref, dscale_ref, mix_ref, nb_ref, nc_ref, nd_ref, av_ref, bufb, bufc, bufd, *, tm, pos0):
    t = pl.program_id(1)
    gw = GROUP_WIDTH
    hb, hc, hd = B_KERNEL - 1, C_KERNEL - 1, POOL_HIST

    def zcols(group, h):
        return z_ref[0, :, group * gw + h * HEAD_DIM:group * gw + (h + 1) * HEAD_DIM]

    @pl.when(t == 0)
    def _():
        for h in range(N_HEADS):
            cs = slice(h * HEAD_DIM, (h + 1) * HEAD_DIM)
            bufb[h, _rows(B_PAD - hb, hb), :] = sb_ref[0, :, cs]
            bufc[h, _rows(C_PAD - hc, hc), :] = sc_ref[0, :, cs]
            bufd[h, _rows(D_PAD - hd, hd), :] = sd_ref[0, :, cs]

    nch = max(tm // A_CHUNK, 1)
    cr = min(tm, A_CHUNK)
    tri = lax.broadcasted_iota(jnp.int32, (cr, cr), 0) >= lax.broadcasted_iota(jnp.int32, (cr, cr), 1)
    pos = pos0 + t * tm + lax.broadcasted_iota(jnp.int32, (tm, HEAD_DIM), 0)

    pooled_minus_token = []
    for h in range(N_HEADS):
        cs = slice(h * HEAD_DIM, (h + 1) * HEAD_DIM)

        wm = jnp.where(tri, ws_ref[h, 0:cr, 0:cr], 0.0).astype(jnp.bfloat16)
        bias = ab_ref[h, 0:cr, :]
        for c in range(nch):
            rs = slice(c * cr, (c + 1) * cr)
            u = _gelu(z_ref[0, rs, h * HEAD_DIM:(h + 1) * HEAD_DIM])
            v = _gelu(z_ref[0, rs, gw + h * HEAD_DIM:gw + (h + 1) * HEAD_DIM])
            if av_ref is not None:
                av_ref[0, rs, cs] = v
            mixv = jnp.dot(wm, v.astype(jnp.bfloat16), preferred_element_type=jnp.float32) + bias
            mix_ref[0, rs, cs] = (u * mixv).astype(mix_ref.dtype)

        bufb[h, _rows(B_PAD, tm), :] = zcols(4, h) * zcols(2, h)
        conv_b = bconv_ref[0:1, cs] * bufb[h, _rows(B_PAD - hb, tm), :]
        for k in range(1, B_KERNEL):
            conv_b = conv_b + bconv_ref[k:k + 1, cs] * bufb[h, _rows(B_PAD - hb + k, tm), :]
        mix_ref[0, :, gw + h * HEAD_DIM:gw + (h + 1) * HEAD_DIM] = (zcols(3, h) * conv_b).astype(mix_ref.dtype)

        bufc[h, _rows(C_PAD, tm), :] = zcols(5, h) * _sigmoid(zcols(6, h))
        conv_c = cconv_ref[0:1, cs] * bufc[h, _rows(C_PAD - hc, tm), :]
        for k in range(1, C_KERNEL):
            conv_c = conv_c + cconv_ref[k:k + 1, cs] * bufc[h, _rows(C_PAD - hc + k, tm), :]
        cf = conv_c + ccb_ref[0:1, cs]
        mu = jnp.mean(cf, axis=-1, keepdims=True)
        cen = cf - mu
        var = jnp.mean(cen * cen, axis=-1, keepdims=True)
        a = cen * lax.rsqrt(var + EPS) * lng_ref[0:1, cs] + lnb_ref[0:1, cs]
        mix_ref[0, :, 2 * gw + h * HEAD_DIM:2 * gw + (h + 1) * HEAD_DIM] = (a * _sigmoid(a)).astype(mix_ref.dtype)

        w = POOL_WINDOWS[h * HEAD_DIM // POOL_GROUP]
        p = zcols(7, h)
        bufd[h, _rows(D_PAD, tm), :] = p
        s = p
        for i in range(1, w):
            s = s + bufd[h, _rows(D_PAD - i, tm), :]
        cnt = jnp.minimum(pos + 1, w).astype(jnp.float32)
        pooled_minus_token.append((s / cnt - p).astype(jnp.bfloat16))

        nb = bufb[h, _rows(tm + B_PAD - hb, hb), :]
        nc = bufc[h, _rows(tm + C_PAD - hc, hc), :]
        nd = bufd[h, _rows(tm + D_PAD - hd, hd), :]
        nb_ref[0, :, cs] = nb
        nc_ref[0, :, cs] = nc
        nd_ref[0, :, cs] = nd
        bufb[h, _rows(B_PAD - hb, hb), :] = nb
        bufc[h, _rows(C_PAD - hc, hc), :] = nc
        bufd[h, _rows(D_PAD - hd, hd), :] = nd

    slabs = POOL_GROUP // HEAD_DIM
    for g in range(len(POOL_WINDOWS)):
        gs = slice(g * POOL_GROUP, (g + 1) * POOL_GROUP)
        d = jnp.concatenate(pooled_minus_token[g * slabs:(g + 1) * slabs], axis=-1)
        yd = jnp.dot(d, dw_ref[g].astype(jnp.bfloat16), preferred_element_type=jnp.float32) * dscale_ref[0:1, gs]
        mix_ref[0, :, 3 * gw + g * POOL_GROUP:3 * gw + (g + 1) * POOL_GROUP] = yd.astype(mix_ref.dtype)


def _mixer_kernel_no_av(*refs, tm, pos0):
    ins, outs, scratch = refs[:13], refs[13:17], refs[17:]
    _mixer_kernel(*ins, *outs, None, *scratch, tm=tm, pos0=pos0)


def _mixers(z, sb, sc, sd, layer, ws, ab_rows, bconv, cconv, ccb, lng, lnb, dw, dscale, *, tm, pos0, want_av):
    b, t, _ = z.shape
    gw = GROUP_WIDTH
    seq = lambda i, j: (layer, i, 0, 0)
    par3 = lambda i, j: (layer, 0, 0)
    par4 = lambda i, j: (layer, 0, 0, 0)
    in_specs = [
        pl.BlockSpec((1, tm, D_IN), lambda i, j: (i, j, 0)),
        pl.BlockSpec((None, 1, B_KERNEL - 1, gw), seq),
        pl.BlockSpec((None, 1, C_KERNEL - 1, gw), seq),
        pl.BlockSpec((None, 1, POOL_HIST, gw), seq),
        pl.BlockSpec((None, N_HEADS, A_CHUNK, A_CHUNK), par4),
        pl.BlockSpec((None, N_HEADS, A_CHUNK, HEAD_DIM), par4),
        pl.BlockSpec((None, B_KERNEL, gw), par3),
        pl.BlockSpec((None, C_KERNEL, gw), par3),
        pl.BlockSpec((None, 1, gw), par3),
        pl.BlockSpec((None, 1, gw), par3),
        pl.BlockSpec((None, 1, gw), par3),
        pl.BlockSpec((None, len(POOL_WINDOWS), POOL_GROUP, POOL_GROUP), par4),
        pl.BlockSpec((None, 1, gw), par3),
    ]
    hist = lambda i, j: (i, 0, 0)
    out_specs = [
        pl.BlockSpec((1, tm, D_MODEL), lambda i, j: (i, j, 0)),
        pl.BlockSpec((1, B_KERNEL - 1, gw), hist),
        pl.BlockSpec((1, C_KERNEL - 1, gw), hist),
        pl.BlockSpec((1, POOL_HIST, gw), hist),
    ]
    out_shape = [
        jax.ShapeDtypeStruct((b, t, D_MODEL), jnp.bfloat16),
        jax.ShapeDtypeStruct((b, B_KERNEL - 1, gw), jnp.float32),
        jax.ShapeDtypeStruct((b, C_KERNEL - 1, gw), jnp.float32),
        jax.ShapeDtypeStruct((b, POOL_HIST, gw), jnp.float32),
    ]
    if want_av:
        out_specs.append(pl.BlockSpec((1, tm, gw), lambda i, j: (i, j, 0)))
        out_shape.append(jax.ShapeDtypeStruct((b, t, gw), jnp.float32))
        body = functools.partial(_mixer_kernel, tm=tm, pos0=pos0)
    else:
        body = functools.partial(_mixer_kernel_no_av, tm=tm, pos0=pos0)
    return pl.pallas_call(
        body,
        grid=(b, t // tm),
        in_specs=in_specs,
        out_specs=out_specs,
        out_shape=out_shape,
        scratch_shapes=[
            pltpu.VMEM((N_HEADS, ROW_STRIDE * (B_PAD + tm), LANES), jnp.float32),
            pltpu.VMEM((N_HEADS, ROW_STRIDE * (C_PAD + tm), LANES), jnp.float32),
            pltpu.VMEM((N_HEADS, ROW_STRIDE * (D_PAD + tm), LANES), jnp.float32),
        ],
        compiler_params=_cparams(("parallel", "arbitrary")),
        name="mixers",
    )(z, sb, sc, sd, ws, ab_rows, bconv, cconv, ccb, lng, lnb, dw, dscale)


def _trunk(x, hist_b, hist_c, hist_d, pos0, want_av, params, big, *, bm, bn, tm):
    (g_mix, w_in, a_ws, ab_rows, b_conv, c_conv, c_conv_b, c_ln_g, c_ln_b, d_w, d_scale, w_out, g_ffn, w_ff1, w_ff2,
     g_final) = params
    b, t, d = x.shape
    m = b * t
    depth = g_mix.shape[0]
    xf = x.reshape(m, d)
    nbs, ncs, nds, avs, copies = [], [], [], [], []

    def mm(lhs, l, which, w_stack, **kw):
        if big is None:
            out, wb = _matmul(lhs, w_stack, l, bm=bm, bn=bn, bk=D_MODEL, **kw)
            copies[l].append(wb)
            return out
        return _matmul(lhs, big[l][which], bm=bm, bn=bn, bk=D_MODEL, **kw)

    for l in range(depth):
        copies.append([])
        h = _rmsnorm(xf, g_mix[l], jnp.bfloat16, bm // 2)
        z = mm(h, l, 0, w_in, out_dtype=jnp.float32, name="w_in")
        outs = _mixers(z.reshape(b, t, D_IN), hist_b, hist_c, hist_d, l, a_ws, ab_rows, b_conv, c_conv, c_conv_b,
                       c_ln_g, c_ln_b, d_w, d_scale, tm=tm, pos0=pos0, want_av=want_av)
        mix = outs[0].reshape(m, D_MODEL)
        nbs.append(outs[1])
        ncs.append(outs[2])
        nds.append(outs[3])
        if want_av:
            avs.append(outs[4])
        xf = mm(mix, l, 1, w_out, out_dtype=jnp.float32, res=xf, name="w_out")
        h = _rmsnorm(xf, g_ffn[l], jnp.bfloat16, bm // 2)
        u = mm(h, l, 2, w_ff1, out_dtype=jnp.bfloat16, relu2=True, name="w_ff1")
        xf = mm(u, l, 3, w_ff2, out_dtype=jnp.float32, res=xf, name="w_ff2")
    y = _rmsnorm(xf, g_final, jnp.float32, bm // 2).reshape(b, t, d)
    return (y, jnp.stack(nbs), jnp.stack(ncs), jnp.stack(nds), (jnp.stack(avs) if want_av else None),
            [tuple(c) for c in copies])


def kernel(x_prompt, x_sample, state_conv_b, state_conv_c, state_pool, g_mix, w_in, a_ws, a_b, b_conv, c_conv,
           c_conv_b, c_ln_g, c_ln_b, d_w, d_scale, w_out, g_ffn, w_ff1, w_ff2, g_final):
    depth = g_mix.shape[0]
    bp = x_prompt.shape[0]
    gw = GROUP_WIDTH
    ab_rows = jnp.broadcast_to(a_b[:, :, :, None], (depth, N_HEADS, A_CHUNK, HEAD_DIM))
    row = lambda p: p.reshape(depth, 1, gw)
    params = (g_mix, w_in, a_ws, ab_rows, b_conv, c_conv, row(c_conv_b), row(c_ln_g), row(c_ln_b), d_w,
              row(d_scale), w_out, g_ffn, w_ff1, w_ff2, g_final)
    zb = jnp.zeros((depth, bp, B_KERNEL - 1, gw), x_prompt.dtype)
    zc = jnp.zeros((depth, bp, C_KERNEL - 1, gw), x_prompt.dtype)
    zd = jnp.zeros((depth, bp, POOL_HIST, gw), x_prompt.dtype)
    ms = x_sample.shape[0] * x_sample.shape[1]
    y_s, nb_s, nc_s, nd_s, av_s, big = _trunk(x_sample, state_conv_b, state_conv_c, state_pool, PAST_LEN, True,
                                              params, None, bm=ms, bn=512, tm=x_sample.shape[1])
    y_p, nb_p, nc_p, nd_p, _, _ = _trunk(x_prompt, zb, zc, zd, 0, False, params, big, bm=1024, bn=1024, tm=A_CHUNK)
    return (y_p, y_s, nb_p, nc_p, nd_p, nb_s, nc_s, nd_s, av_s)
```

```python
import functools
import math

import jax
import jax.numpy as jnp
from jax import lax
from jax.experimental import pallas as pl
from jax.experimental.pallas import tpu as pltpu

D_MODEL = 4096
GROUP_WIDTH = D_MODEL // 4
D_IN = 8 * GROUP_WIDTH
D_FF = 4 * D_MODEL
A_CHUNK = 128
HEAD_DIM = 128
N_HEADS = GROUP_WIDTH // HEAD_DIM
B_KERNEL = 3
C_KERNEL = 31
POOL_WINDOWS = (2, 4, 8, 16)
POOL_GROUP = GROUP_WIDTH // len(POOL_WINDOWS)
POOL_HIST = max(POOL_WINDOWS) - 1
PAST_LEN = 2048
EPS = 1e-6

SUBLANES = 8
LANES = 128
VMEM_LIMIT_BYTES = 62 * 1024 * 1024
N_ROW_TILES = 8
NORM_ROWS = 512

ROW_STRIDE = 2
B_PAD = SUBLANES
C_PAD = 4 * SUBLANES
D_PAD = 2 * SUBLANES


def _cparams(semantics):
    return pltpu.CompilerParams(dimension_semantics=semantics, vmem_limit_bytes=VMEM_LIMIT_BYTES)


def _rmsnorm_kernel(x_ref, g_ref, o_ref):
    x = x_ref[...]
    ms = jnp.mean(x * x, axis=-1, keepdims=True)
    o_ref[...] = (x * lax.rsqrt(ms + EPS) * g_ref[...]).astype(o_ref.dtype)


def _rmsnorm(x, g, out_dtype, *, block0=0, nblocks=None):
    m, d = x.shape
    nblocks = m // NORM_ROWS if nblocks is None else nblocks
    return pl.pallas_call(
        _rmsnorm_kernel,
        grid=(nblocks,),
        in_specs=[pl.BlockSpec((NORM_ROWS, d), lambda i: (i + block0, 0)), pl.BlockSpec((1, d), lambda i: (0, 0))],
        out_specs=pl.BlockSpec((NORM_ROWS, d), lambda i: (i, 0)),
        out_shape=jax.ShapeDtypeStruct((nblocks * NORM_ROWS, d), out_dtype),
        compiler_params=_cparams(("parallel",)),
        name="rmsnorm",
    )(x, g.reshape(1, d))


def _matmul_kernel(*refs, nk, kdim, relu2, has_res, emit_bf16, scale_rows, emit_norm, n_alias):
    it = iter(refs)
    x_ref, w_ref = next(it), next(it)
    res_ref = next(it) if has_res else None
    ssq_in_ref = next(it) if scale_rows else None
    g_ref = next(it) if emit_norm else None
    for _ in range(n_alias):
        next(it)
    o_ref = next(it)
    wb_ref = next(it) if emit_bf16 else None
    xg_ref, ssq_ref = (next(it), next(it)) if emit_norm else (None, None)
    j, k = pl.program_id(1), pl.program_id(2)

    w = w_ref[...]
    if emit_bf16:
        w = w.astype(jnp.bfloat16)
        wb_ref[...] = w
    part = jnp.dot(x_ref[...], w, preferred_element_type=jnp.float32)
    if scale_rows:
        assert nk == 1
        ms = jnp.sum(ssq_in_ref[...], axis=-1, keepdims=True) * (1.0 / kdim)
        part = part * lax.rsqrt(ms + EPS)
    if relu2:
        assert nk == 1
        r = jnp.maximum(part, 0.0)
        part = r * r

    def emit(v):
        xg_ref[...] = (v * g_ref[...]).astype(xg_ref.dtype)
        sq = v * v
        lanes = sq[:, 0:LANES]
        for c in range(1, sq.shape[1] // LANES):
            lanes = lanes + sq[:, c * LANES:(c + 1) * LANES]

        @pl.when(j == 0)
        def _():
            ssq_ref[...] = lanes

        @pl.when(j > 0)
        def _():
            ssq_ref[...] += lanes

    if nk == 1:
        if has_res:
            part = res_ref[...] + part
        o_ref[...] = part.astype(o_ref.dtype)
        if emit_norm:
            emit(part)
    else:
        assert o_ref.dtype == jnp.float32

        @pl.when(k == 0)
        def _():
            o_ref[...] = res_ref[...] if has_res else jnp.zeros_like(o_ref)

        o_ref[...] += part
        if emit_norm:
            @pl.when(k == nk - 1)
            def _():
                emit(o_ref[...])


def _matmul_call(x, w, *, layer, tile0, ntiles, prev, bm, bn, bk, out_dtype, relu2, res, ssq, norm_g, name):
    m, kdim = x.shape
    n = w.shape[-1]
    nk = kdim // bk
    emit_bf16 = layer is not None
    rows = lambda i, j, k: (i + tile0, j)
    if emit_bf16:
        assert ntiles == 1 and w.dtype == jnp.float32
        w_spec = pl.BlockSpec((None, bk, bn), lambda i, j, k: (layer, k, j))
    else:
        assert w.dtype == jnp.bfloat16
        w_spec = pl.BlockSpec((bk, bn), lambda i, j, k: (k, j))
    in_specs = [pl.BlockSpec((bm, bk), lambda i, j, k: (i + tile0, k)), w_spec]
    args = [x, w]
    if res is not None:
        in_specs.append(pl.BlockSpec((bm, bn), rows))
        args.append(res)
    if ssq is not None:
        in_specs.append(pl.BlockSpec((bm, LANES), lambda i, j, k: (i + tile0, 0)))
        args.append(ssq)
    if norm_g is not None:
        in_specs.append(pl.BlockSpec((1, bn), lambda i, j, k: (0, j)))
        args.append(norm_g.reshape(1, n))
    out_specs = [pl.BlockSpec((bm, bn), rows)]
    out_shape = [jax.ShapeDtypeStruct((m, n), out_dtype)]
    if emit_bf16:
        out_specs.append(pl.BlockSpec((bk, bn), lambda i, j, k: (k, j)))
        out_shape.append(jax.ShapeDtypeStruct((kdim, n), jnp.bfloat16))
    if norm_g is not None:
        out_specs += [pl.BlockSpec((bm, bn), rows), pl.BlockSpec((bm, LANES), lambda i, j, k: (i + tile0, 0))]
        out_shape += [jax.ShapeDtypeStruct((m, n), jnp.bfloat16), jax.ShapeDtypeStruct((m, LANES), jnp.float32)]
    aliases = {}
    prev = [] if prev is None else prev
    for out_index, arr in enumerate(prev):
        aliases[len(args)] = out_index
        in_specs.append(pl.BlockSpec(memory_space=pl.ANY))
        args.append(arr)
    return pl.pallas_call(
        functools.partial(_matmul_kernel, nk=nk, kdim=kdim, relu2=relu2, has_res=res is not None,
                          emit_bf16=emit_bf16, scale_rows=ssq is not None, emit_norm=norm_g is not None,
                          n_alias=len(prev)),
        grid=(ntiles, n // bn, nk),
        in_specs=in_specs,
        out_specs=out_specs,
        out_shape=out_shape,
        input_output_aliases=aliases,
        compiler_params=_cparams(("parallel", "arbitrary", "arbitrary")),
        name=name,
    )(*args)


def _matmul(x, w_stack, layer, *, first, rest, out_dtype, relu2=False, res=None, ssq=None, norm_g=None, name):
    m = x.shape[0]
    bm = m // N_ROW_TILES
    assert bm * N_ROW_TILES == m and bm % (2 * SUBLANES) == 0
    common = dict(bm=bm, out_dtype=out_dtype, relu2=relu2, res=res, ssq=ssq, norm_g=norm_g)
    outs = _matmul_call(x, w_stack, layer=layer, tile0=0, ntiles=1, prev=None, bn=first[0], bk=first[1],
                        name=name + "_first", **common)
    wb = outs[1]
    keep = [outs[0]] + list(outs[2:])
    outs = _matmul_call(x, wb, layer=None, tile0=1, ntiles=N_ROW_TILES - 1, prev=keep, bn=rest[0], bk=rest[1],
                        name=name, **common)
    return outs[0] if norm_g is None else tuple(outs)


TILES = {
    "w_in": dict(first=(512, D_MODEL), rest=(1024, D_MODEL)),
    "w_out": dict(first=(512, D_MODEL), rest=(512, D_MODEL)),
    "w_ff1": dict(first=(512, D_MODEL), rest=(1024, D_MODEL)),
    "w_ff2": dict(first=(1024, D_MODEL // 2), rest=(1024, D_MODEL)),
}


def _gelu(x):
    return 0.5 * x * (1.0 + lax.erf(x * math.sqrt(0.5)))


def _sigmoid(x):
    return 1.0 / (1.0 + jnp.exp(-x))


def _rows(first, count):
    return pl.ds(ROW_STRIDE * first, count, stride=ROW_STRIDE)


def _mixer_kernel(*refs, tm, pos0, want_av, has_alias):
    (z_ref, sb_ref, sc_ref, sd_ref, ws_ref, ab_ref, bconv_ref, cconv_ref, ccb_ref, lng_ref, lnb_ref, dw_ref,
     dscale_ref) = refs[:13]
    outs = refs[13 + has_alias:]
    mix_ref, nb_ref, nc_ref, nd_ref = outs[:4]
    av_ref = outs[4] if want_av else None
    bufb, bufc, bufd = outs[4 + want_av:]
    t = pl.program_id(1)
    gw = GROUP_WIDTH
    hb, hc, hd = B_KERNEL - 1, C_KERNEL - 1, POOL_HIST

    def zcols(group, h):
        return z_ref[:, group * gw + h * HEAD_DIM:group * gw + (h + 1) * HEAD_DIM]

    @pl.when(t == 0)
    def _():
        for h in range(N_HEADS):
            cs = slice(h * HEAD_DIM, (h + 1) * HEAD_DIM)
            bufb[h, _rows(B_PAD - hb, hb), :] = sb_ref[0, :, cs]
            bufc[h, _rows(C_PAD - hc, hc), :] = sc_ref[0, :, cs]
            bufd[h, _rows(D_PAD - hd, hd), :] = sd_ref[0, :, cs]

    nch = max(tm // A_CHUNK, 1)
    cr = min(tm, A_CHUNK)
    tri = lax.broadcasted_iota(jnp.int32, (cr, cr), 0) >= lax.broadcasted_iota(jnp.int32, (cr, cr), 1)
    pos = pos0 + t * tm + lax.broadcasted_iota(jnp.int32, (tm, HEAD_DIM), 0)

    pooled_minus_token = []
    for h in range(N_HEADS):
        cs = slice(h * HEAD_DIM, (h + 1) * HEAD_DIM)

        wm = jnp.where(tri, ws_ref[h, 0:cr, 0:cr], 0.0).astype(jnp.bfloat16)
        bias = ab_ref[h, 0:cr, :]
        for c in range(nch):
            rs = slice(c * cr, (c + 1) * cr)
            u = _gelu(z_ref[rs, h * HEAD_DIM:(h + 1) * HEAD_DIM])
            v = _gelu(z_ref[rs, gw + h * HEAD_DIM:gw + (h + 1) * HEAD_DIM])
            if av_ref is not None:
                av_ref[0, rs, cs] = v
            mixv = jnp.dot(wm, v.astype(jnp.bfloat16), preferred_element_type=jnp.float32) + bias
            mix_ref[rs, cs] = (u * mixv).astype(mix_ref.dtype)

        bufb[h, _rows(B_PAD, tm), :] = zcols(4, h) * zcols(2, h)
        conv_b = bconv_ref[0:1, cs] * bufb[h, _rows(B_PAD - hb, tm), :]
        for k in range(1, B_KERNEL):
            conv_b = conv_b + bconv_ref[k:k + 1, cs] * bufb[h, _rows(B_PAD - hb + k, tm), :]
        mix_ref[:, gw + h * HEAD_DIM:gw + (h + 1) * HEAD_DIM] = (zcols(3, h) * conv_b).astype(mix_ref.dtype)

        bufc[h, _rows(C_PAD, tm), :] = zcols(5, h) * _sigmoid(zcols(6, h))
        conv_c = cconv_ref[0:1, cs] * bufc[h, _rows(C_PAD - hc, tm), :]
        for k in range(1, C_KERNEL):
            conv_c = conv_c + cconv_ref[k:k + 1, cs] * bufc[h, _rows(C_PAD - hc + k, tm), :]
        cf = conv_c + ccb_ref[0:1, cs]
        mu = jnp.mean(cf, axis=-1, keepdims=True)
        cen = cf - mu
        var = jnp.mean(cen * cen, axis=-1, keepdims=True)
        a = cen * lax.rsqrt(var + EPS) * lng_ref[0:1, cs] + lnb_ref[0:1, cs]
        mix_ref[:, 2 * gw + h * HEAD_DIM:2 * gw + (h + 1) * HEAD_DIM] = (a * _sigmoid(a)).astype(mix_ref.dtype)

        w = POOL_WINDOWS[h * HEAD_DIM // POOL_GROUP]
        p = zcols(7, h)
        bufd[h, _rows(D_PAD, tm), :] = p
        s = p
        for i in range(1, w):
            s = s + bufd[h, _rows(D_PAD - i, tm), :]
        cnt = jnp.minimum(pos + 1, w).astype(jnp.float32)
        pooled_minus_token.append((s / cnt - p).astype(jnp.bfloat16))

        nb = bufb[h, _rows(tm + B_PAD - hb, hb), :]
        nc = bufc[h, _rows(tm + C_PAD - hc, hc), :]
        nd = bufd[h, _rows(tm + D_PAD - hd, hd), :]
        nb_ref[0, :, cs] = nb
        nc_ref[0, :, cs] = nc
        nd_ref[0, :, cs] = nd
        bufb[h, _rows(B_PAD - hb, hb), :] = nb
        bufc[h, _rows(C_PAD - hc, hc), :] = nc
        bufd[h, _rows(D_PAD - hd, hd), :] = nd

    slabs = POOL_GROUP // HEAD_DIM
    for g in range(len(POOL_WINDOWS)):
        gs = slice(g * POOL_GROUP, (g + 1) * POOL_GROUP)
        d = jnp.concatenate(pooled_minus_token[g * slabs:(g + 1) * slabs], axis=-1)
        yd = jnp.dot(d, dw_ref[g].astype(jnp.bfloat16), preferred_element_type=jnp.float32) * dscale_ref[0:1, gs]
        mix_ref[:, 3 * gw + g * POOL_GROUP:3 * gw + (g + 1) * POOL_GROUP] = yd.astype(mix_ref.dtype)


def _mixers(z, sb, sc, sd, layer, small, *, nseq, seqlen, row0, tm, pos0, want_av, mix_prev):
    rows = z.shape[0]
    gw = GROUP_WIDTH
    tiles = seqlen // tm
    block0 = row0 // tm
    assert tiles * tm == seqlen and block0 * tm == row0
    tile = lambda i, j: (block0 + i * tiles + j, 0)
    seq = lambda i, j: (layer, i, 0, 0)
    par3 = lambda i, j: (layer, 0, 0)
    par4 = lambda i, j: (layer, 0, 0, 0)
    in_specs = [
        pl.BlockSpec((tm, D_IN), tile),
        pl.BlockSpec((None, 1, B_KERNEL - 1, gw), seq),
        pl.BlockSpec((None, 1, C_KERNEL - 1, gw), seq),
        pl.BlockSpec((None, 1, POOL_HIST, gw), seq),
        pl.BlockSpec((None, N_HEADS, A_CHUNK, A_CHUNK), par4),
        pl.BlockSpec((None, N_HEADS, A_CHUNK, HEAD_DIM), par4),
        pl.BlockSpec((None, B_KERNEL, gw), par3),
        pl.BlockSpec((None, C_KERNEL, gw), par3),
        pl.BlockSpec((None, 1, gw), par3),
        pl.BlockSpec((None, 1, gw), par3),
        pl.BlockSpec((None, 1, gw), par3),
        pl.BlockSpec((None, len(POOL_WINDOWS), POOL_GROUP, POOL_GROUP), par4),
        pl.BlockSpec((None, 1, gw), par3),
    ]
    args = [z, sb, sc, sd, *small]
    aliases = {}
    if mix_prev is not None:
        aliases[len(args)] = 0
        in_specs.append(pl.BlockSpec(memory_space=pl.ANY))
        args.append(mix_prev)
    hist = lambda i, j: (i, 0, 0)
    out_specs = [
        pl.BlockSpec((tm, D_MODEL), tile),
        pl.BlockSpec((1, B_KERNEL - 1, gw), hist),
        pl.BlockSpec((1, C_KERNEL - 1, gw), hist),
        pl.BlockSpec((1, POOL_HIST, gw), hist),
    ]
    out_shape = [
        jax.ShapeDtypeStruct((rows, D_MODEL), jnp.bfloat16),
        jax.ShapeDtypeStruct((nseq, B_KERNEL - 1, gw), jnp.float32),
        jax.ShapeDtypeStruct((nseq, C_KERNEL - 1, gw), jnp.float32),
        jax.ShapeDtypeStruct((nseq, POOL_HIST, gw), jnp.float32),
    ]
    if want_av:
        out_specs.append(pl.BlockSpec((1, tm, gw), lambda i, j: (i, j, 0)))
        out_shape.append(jax.ShapeDtypeStruct((nseq, seqlen, gw), jnp.float32))
    return pl.pallas_call(
        functools.partial(_mixer_kernel, tm=tm, pos0=pos0, want_av=want_av, has_alias=mix_prev is not None),
        grid=(nseq, tiles),
        in_specs=in_specs,
        out_specs=out_specs,
        out_shape=out_shape,
        input_output_aliases=aliases,
        scratch_shapes=[
            pltpu.VMEM((N_HEADS, ROW_STRIDE * (B_PAD + tm), LANES), jnp.float32),
            pltpu.VMEM((N_HEADS, ROW_STRIDE * (C_PAD + tm), LANES), jnp.float32),
            pltpu.VMEM((N_HEADS, ROW_STRIDE * (D_PAD + tm), LANES), jnp.float32),
        ],
        compiler_params=_cparams(("parallel", "arbitrary")),
        name="mixers",
    )(*args)


def kernel(x_prompt, x_sample, state_conv_b, state_conv_c, state_pool, g_mix, w_in, a_ws, a_b, b_conv, c_conv,
           c_conv_b, c_ln_g, c_ln_b, d_w, d_scale, w_out, g_ffn, w_ff1, w_ff2, g_final):
    depth = g_mix.shape[0]
    bp, tp, d = x_prompt.shape
    bs, ts, _ = x_sample.shape
    rows_p, rows_s = bp * tp, bs * ts
    gw = GROUP_WIDTH
    ab_rows = jnp.broadcast_to(a_b[:, :, :, None], (depth, N_HEADS, A_CHUNK, HEAD_DIM))
    row = lambda p: p.reshape(depth, 1, gw)
    small = (a_ws, ab_rows, b_conv, c_conv, row(c_conv_b), row(c_ln_g), row(c_ln_b), d_w, row(d_scale))
    zb = jnp.zeros((depth, bp, B_KERNEL - 1, gw), x_prompt.dtype)
    zc = jnp.zeros((depth, bp, C_KERNEL - 1, gw), x_prompt.dtype)
    zd = jnp.zeros((depth, bp, POOL_HIST, gw), x_prompt.dtype)

    x = jnp.concatenate([x_prompt.reshape(rows_p, d), x_sample.reshape(rows_s, d)], axis=0)
    new_p, new_s, av_s = [], [], []
    for l in range(depth):
        h = _rmsnorm(x, g_mix[l], jnp.bfloat16)
        z = _matmul(h, w_in, l, out_dtype=jnp.float32, name="w_in", **TILES["w_in"])
        outs_p = _mixers(z, zb, zc, zd, l, small, nseq=bp, seqlen=tp, row0=0, tm=A_CHUNK, pos0=0, want_av=False,
                         mix_prev=None)
        outs_s = _mixers(z, state_conv_b, state_conv_c, state_pool, l, small, nseq=bs, seqlen=ts, row0=rows_p,
                         tm=ts, pos0=PAST_LEN, want_av=True, mix_prev=outs_p[0])
        new_p.append(outs_p[1:4])
        new_s.append(outs_s[1:4])
        av_s.append(outs_s[4])
        x, xg, ssq = _matmul(outs_s[0], w_out, l, out_dtype=jnp.float32, res=x, norm_g=g_ffn[l], name="w_out",
                             **TILES["w_out"])
        u = _matmul(xg, w_ff1, l, out_dtype=jnp.bfloat16, relu2=True, ssq=ssq, name="w_ff1", **TILES["w_ff1"])
        x = _matmul(u, w_ff2, l, out_dtype=jnp.float32, res=x, name="w_ff2", **TILES["w_ff2"])
    y_p = _rmsnorm(x, g_final, jnp.float32, block0=0, nblocks=rows_p // NORM_ROWS).reshape(bp, tp, d)
    y_s = _rmsnorm(x, g_final, jnp.float32, block0=rows_p // NORM_ROWS, nblocks=rows_s // NORM_ROWS).reshape(bs, ts, d)
    stack = lambda per_layer, which: jnp.stack([o[which] for o in per_layer])
    return (y_p, y_s, stack(new_p, 0), stack(new_p, 1), stack(new_p, 2), stack(new_s, 0), stack(new_s, 1),
            stack(new_s, 2), jnp.stack(av_s))
```

```python
import functools
import math

import jax
import jax.numpy as jnp
from jax import lax
from jax.experimental import pallas as pl
from jax.experimental.pallas import tpu as pltpu

D_MODEL = 4096
GROUP_WIDTH = D_MODEL // 4
D_IN = 8 * GROUP_WIDTH
D_FF = 4 * D_MODEL
A_CHUNK = 128
HEAD_DIM = 128
N_HEADS = GROUP_WIDTH // HEAD_DIM
B_KERNEL = 3
C_KERNEL = 31
POOL_WINDOWS = (2, 4, 8, 16)
POOL_GROUP = GROUP_WIDTH // len(POOL_WINDOWS)
POOL_HIST = max(POOL_WINDOWS) - 1
PAST_LEN = 2048
EPS = 1e-6

SUBLANES = 8
LANES = 128
VMEM_LIMIT_BYTES = 62 * 1024 * 1024
N_ROW_TILES = 8
NORM_ROWS = 512

ROW_STRIDE = 2
B_PAD = SUBLANES
C_PAD = 4 * SUBLANES
D_PAD = 2 * SUBLANES


def _cparams(semantics):
    return pltpu.CompilerParams(dimension_semantics=semantics, vmem_limit_bytes=VMEM_LIMIT_BYTES)


def _rmsnorm_kernel(x_ref, g_ref, o_ref):
    x = x_ref[...]
    ms = jnp.mean(x * x, axis=-1, keepdims=True)
    o_ref[...] = (x * lax.rsqrt(ms + EPS) * g_ref[...]).astype(o_ref.dtype)


def _rmsnorm(x, g, out_dtype, *, block0=0, nblocks=None):
    m, d = x.shape
    nblocks = m // NORM_ROWS if nblocks is None else nblocks
    return pl.pallas_call(
        _rmsnorm_kernel,
        grid=(nblocks,),
        in_specs=[pl.BlockSpec((NORM_ROWS, d), lambda i: (i + block0, 0)), pl.BlockSpec((1, d), lambda i: (0, 0))],
        out_specs=pl.BlockSpec((NORM_ROWS, d), lambda i: (i, 0)),
        out_shape=jax.ShapeDtypeStruct((nblocks * NORM_ROWS, d), out_dtype),
        compiler_params=_cparams(("parallel",)),
        name="rmsnorm",
    )(x, g.reshape(1, d))


def _gather_norm_kernel(*refs):
    x_ref, g_ref = refs[0], refs[1]
    x_out_ref, h_ref = refs[-2], refs[-1]
    x = x_ref[...]
    x_out_ref[...] = x
    ms = jnp.mean(x * x, axis=-1, keepdims=True)
    h_ref[...] = (x * lax.rsqrt(ms + EPS) * g_ref[...]).astype(h_ref.dtype)


def _gather_norm(parts, g):
    d = parts[0].shape[1]
    rows = sum(p.shape[0] for p in parts)
    outs, block0 = None, 0
    for p in parts:
        nblocks = p.shape[0] // NORM_ROWS
        assert nblocks * NORM_ROWS == p.shape[0]
        in_specs = [pl.BlockSpec((NORM_ROWS, d), lambda i: (i, 0)), pl.BlockSpec((1, d), lambda i: (0, 0))]
        args = [p, g.reshape(1, d)]
        aliases = {}
        if outs is not None:
            aliases = {2: 0, 3: 1}
            in_specs += [pl.BlockSpec(memory_space=pl.ANY)] * 2
            args += list(outs)
        out_spec = pl.BlockSpec((NORM_ROWS, d), lambda i, block0=block0: (i + block0, 0))
        outs = pl.pallas_call(
            _gather_norm_kernel,
            grid=(nblocks,),
            in_specs=in_specs,
            out_specs=[out_spec, out_spec],
            out_shape=[jax.ShapeDtypeStruct((rows, d), jnp.float32), jax.ShapeDtypeStruct((rows, d), jnp.bfloat16)],
            input_output_aliases=aliases,
            compiler_params=_cparams(("parallel",)),
            name="gather_norm",
        )(*args)
        block0 += nblocks
    return outs


def _matmul_kernel(*refs, nk, kdim, relu2, has_res, emit_bf16, scale_rows, emit_norm, n_alias):
    it = iter(refs)
    x_ref, w_ref = next(it), next(it)
    res_ref = next(it) if has_res else None
    ssq_in_ref = next(it) if scale_rows else None
    g_ref = next(it) if emit_norm else None
    for _ in range(n_alias):
        next(it)
    o_ref = next(it)
    wb_ref = next(it) if emit_bf16 else None
    xg_ref, ssq_ref = (next(it), next(it)) if emit_norm else (None, None)
    j, k = pl.program_id(1), pl.program_id(2)

    if emit_bf16:
        wb_ref[...] = w_ref[...].astype(jnp.bfloat16)
        w_ref = wb_ref

    def product():
        return jnp.dot(x_ref[...], w_ref[...], preferred_element_type=jnp.float32)

    def emit(v):
        xg_ref[...] = (v * g_ref[...]).astype(xg_ref.dtype)
        sq = v * v
        lanes = sq[:, 0:LANES]
        for c in range(1, sq.shape[1] // LANES):
            lanes = lanes + sq[:, c * LANES:(c + 1) * LANES]
        ssq_ref[...] += lanes

    if emit_norm:
        @pl.when((j == 0) & (k == 0))
        def _():
            ssq_ref[...] = jnp.zeros_like(ssq_ref)

    if nk == 1:
        part = product()
        if scale_rows:
            ms = jnp.sum(ssq_in_ref[...], axis=-1, keepdims=True) * (1.0 / kdim)
            part = part * lax.rsqrt(ms + EPS)
        if relu2:
            r = jnp.maximum(part, 0.0)
            part = r * r
        if has_res:
            part = res_ref[...] + part
        o_ref[...] = part.astype(o_ref.dtype)
        if emit_norm:
            emit(part)
    else:
        assert o_ref.dtype == jnp.float32 and not scale_rows and not relu2

        @pl.when(k == 0)
        def _():
            o_ref[...] = (res_ref[...] + product()) if has_res else product()

        @pl.when(k > 0)
        def _():
            o_ref[...] = o_ref[...] + product()

        if emit_norm:
            @pl.when(k == nk - 1)
            def _():
                emit(o_ref[...])


def _matmul_call(x, w, *, layer, tile0, ntiles, prev, bm, bn, bk, out_dtype, relu2, res, ssq, norm_g, name):
    m, kdim = x.shape
    n = w.shape[-1]
    nk = kdim // bk
    emit_bf16 = layer is not None
    rows = lambda i, j, k: (i + tile0, j)
    if emit_bf16:
        assert ntiles == 1 and w.dtype == jnp.float32
        w_spec = pl.BlockSpec((None, bk, bn), lambda i, j, k: (layer, k, j))
    else:
        assert w.dtype == jnp.bfloat16
        w_spec = pl.BlockSpec((bk, bn), lambda i, j, k: (k, j))
    in_specs = [pl.BlockSpec((bm, bk), lambda i, j, k: (i + tile0, k)), w_spec]
    args = [x, w]
    if res is not None:
        in_specs.append(pl.BlockSpec((bm, bn), rows))
        args.append(res)
    if ssq is not None:
        in_specs.append(pl.BlockSpec((bm, LANES), lambda i, j, k: (i + tile0, 0)))
        args.append(ssq)
    if norm_g is not None:
        in_specs.append(pl.BlockSpec((1, bn), lambda i, j, k: (0, j)))
        args.append(norm_g.reshape(1, n))
    out_specs = [pl.BlockSpec((bm, bn), rows)]
    out_shape = [jax.ShapeDtypeStruct((m, n), out_dtype)]
    if emit_bf16:
        out_specs.append(pl.BlockSpec((bk, bn), lambda i, j, k: (k, j)))
        out_shape.append(jax.ShapeDtypeStruct((kdim, n), jnp.bfloat16))
    if norm_g is not None:
        out_specs += [pl.BlockSpec((bm, bn), rows), pl.BlockSpec((bm, LANES), lambda i, j, k: (i + tile0, 0))]
        out_shape += [jax.ShapeDtypeStruct((m, n), jnp.bfloat16), jax.ShapeDtypeStruct((m, LANES), jnp.float32)]
    aliases = {}
    prev = [] if prev is None else prev
    for out_index, arr in enumerate(prev):
        aliases[len(args)] = out_index
        in_specs.append(pl.BlockSpec(memory_space=pl.ANY))
        args.append(arr)
    return pl.pallas_call(
        functools.partial(_matmul_kernel, nk=nk, kdim=kdim, relu2=relu2, has_res=res is not None,
                          emit_bf16=emit_bf16, scale_rows=ssq is not None, emit_norm=norm_g is not None,
                          n_alias=len(prev)),
        grid=(ntiles, n // bn, nk),
        in_specs=in_specs,
        out_specs=out_specs,
        out_shape=out_shape,
        input_output_aliases=aliases,
        compiler_params=_cparams(("parallel", "arbitrary", "arbitrary")),
        name=name,
    )(*args)


def _matmul(x, w_stack, layer, *, first, rest, out_dtype, relu2=False, res=None, ssq=None, norm_g=None, name):
    m = x.shape[0]
    bm = m // N_ROW_TILES
    assert bm * N_ROW_TILES == m and bm % (2 * SUBLANES) == 0
    common = dict(bm=bm, out_dtype=out_dtype, relu2=relu2, res=res, ssq=ssq, norm_g=norm_g)
    outs = _matmul_call(x, w_stack, layer=layer, tile0=0, ntiles=1, prev=None, bn=first[0], bk=first[1],
                        name=name + "_first", **common)
    wb = outs[1]
    keep = [outs[0]] + list(outs[2:])
    outs = _matmul_call(x, wb, layer=None, tile0=1, ntiles=N_ROW_TILES - 1, prev=keep, bn=rest[0], bk=rest[1],
                        name=name, **common)
    return outs[0] if norm_g is None else tuple(outs)


TILES = {
    "w_in": dict(first=(512, D_MODEL), rest=(1024, D_MODEL)),
    "w_out": dict(first=(512, D_MODEL), rest=(512, D_MODEL)),
    "w_ff1": dict(first=(512, D_MODEL), rest=(1024, D_MODEL)),
    "w_ff2": dict(first=(1024, D_MODEL // 2), rest=(1024, D_MODEL)),
}


def _gelu(x):
    return 0.5 * x * (1.0 + lax.erf(x * math.sqrt(0.5)))


def _sigmoid(x):
    return 1.0 / (1.0 + jnp.exp(-x))


def _rows(first, count):
    return pl.ds(ROW_STRIDE * first, count, stride=ROW_STRIDE)


def _mixer_kernel(*refs, tm, pos0, want_av, has_alias):
    (z_ref, sb_ref, sc_ref, sd_ref, ws_ref, ab_ref, bconv_ref, cconv_ref, ccb_ref, lng_ref, lnb_ref, dw_ref,
     dscale_ref) = refs[:13]
    outs = refs[13 + has_alias:]
    mix_ref, nb_ref, nc_ref, nd_ref = outs[:4]
    av_ref = outs[4] if want_av else None
    bufb, bufc, bufd = outs[4 + want_av:]
    t = pl.program_id(1)
    gw = GROUP_WIDTH
    hb, hc, hd = B_KERNEL - 1, C_KERNEL - 1, POOL_HIST

    def zcols(group, h):
        return z_ref[:, group * gw + h * HEAD_DIM:group * gw + (h + 1) * HEAD_DIM]

    @pl.when(t == 0)
    def _():
        for h in range(N_HEADS):
            cs = slice(h * HEAD_DIM, (h + 1) * HEAD_DIM)
            bufb[h, _rows(B_PAD - hb, hb), :] = sb_ref[0, :, cs]
            bufc[h, _rows(C_PAD - hc, hc), :] = sc_ref[0, :, cs]
            bufd[h, _rows(D_PAD - hd, hd), :] = sd_ref[0, :, cs]

    nch = max(tm // A_CHUNK, 1)
    cr = min(tm, A_CHUNK)
    tri = lax.broadcasted_iota(jnp.int32, (cr, cr), 0) >= lax.broadcasted_iota(jnp.int32, (cr, cr), 1)
    pos = pos0 + t * tm + lax.broadcasted_iota(jnp.int32, (tm, HEAD_DIM), 0)

    pooled_minus_token = []
    for h in range(N_HEADS):
        cs = slice(h * HEAD_DIM, (h + 1) * HEAD_DIM)

        wm = jnp.where(tri, ws_ref[h, 0:cr, 0:cr], 0.0).astype(jnp.bfloat16)
        bias = ab_ref[h, 0:cr, :]
        for c in range(nch):
            rs = slice(c * cr, (c + 1) * cr)
            u = _gelu(z_ref[rs, h * HEAD_DIM:(h + 1) * HEAD_DIM])
            v = _gelu(z_ref[rs, gw + h * HEAD_DIM:gw + (h + 1) * HEAD_DIM])
            if av_ref is not None:
                av_ref[0, rs, cs] = v
            mixv = jnp.dot(wm, v.astype(jnp.bfloat16), preferred_element_type=jnp.float32) + bias
            mix_ref[rs, cs] = (u * mixv).astype(mix_ref.dtype)

        bufb[h, _rows(B_PAD, tm), :] = zcols(4, h) * zcols(2, h)
        conv_b = bconv_ref[0:1, cs] * bufb[h, _rows(B_PAD - hb, tm), :]
        for k in range(1, B_KERNEL):
            conv_b = conv_b + bconv_ref[k:k + 1, cs] * bufb[h, _rows(B_PAD - hb + k, tm), :]
        mix_ref[:, gw + h * HEAD_DIM:gw + (h + 1) * HEAD_DIM] = (zcols(3, h) * conv_b).astype(mix_ref.dtype)

        bufc[h, _rows(C_PAD, tm), :] = zcols(5, h) * _sigmoid(zcols(6, h))
        conv_c = cconv_ref[0:1, cs] * bufc[h, _rows(C_PAD - hc, tm), :]
        for k in range(1, C_KERNEL):
            conv_c = conv_c + cconv_ref[k:k + 1, cs] * bufc[h, _rows(C_PAD - hc + k, tm), :]
        cf = conv_c + ccb_ref[0:1, cs]
        mu = jnp.mean(cf, axis=-1, keepdims=True)
        cen = cf - mu
        var = jnp.mean(cen * cen, axis=-1, keepdims=True)
        a = cen * lax.rsqrt(var + EPS) * lng_ref[0:1, cs] + lnb_ref[0:1, cs]
        mix_ref[:, 2 * gw + h * HEAD_DIM:2 * gw + (h + 1) * HEAD_DIM] = (a * _sigmoid(a)).astype(mix_ref.dtype)

        w = POOL_WINDOWS[h * HEAD_DIM // POOL_GROUP]
        p = zcols(7, h)
        bufd[h, _rows(D_PAD, tm), :] = p
        s = p
        for i in range(1, w):
            s = s + bufd[h, _rows(D_PAD - i, tm), :]
        cnt = jnp.minimum(pos + 1, w).astype(jnp.float32)
        pooled_minus_token.append((s / cnt - p).astype(jnp.bfloat16))

        nb = bufb[h, _rows(tm + B_PAD - hb, hb), :]
        nc = bufc[h, _rows(tm + C_PAD - hc, hc), :]
        nd = bufd[h, _rows(tm + D_PAD - hd, hd), :]
        nb_ref[0, :, cs] = nb
        nc_ref[0, :, cs] = nc
        nd_ref[0, :, cs] = nd
        bufb[h, _rows(B_PAD - hb, hb), :] = nb
        bufc[h, _rows(C_PAD - hc, hc), :] = nc
        bufd[h, _rows(D_PAD - hd, hd), :] = nd

    slabs = POOL_GROUP // HEAD_DIM
    for g in range(len(POOL_WINDOWS)):
        gs = slice(g * POOL_GROUP, (g + 1) * POOL_GROUP)
        d = jnp.concatenate(pooled_minus_token[g * slabs:(g + 1) * slabs], axis=-1)
        yd = jnp.dot(d, dw_ref[g].astype(jnp.bfloat16), preferred_element_type=jnp.float32) * dscale_ref[0:1, gs]
        mix_ref[:, 3 * gw + g * POOL_GROUP:3 * gw + (g + 1) * POOL_GROUP] = yd.astype(mix_ref.dtype)


def _mixers(z, sb, sc, sd, layer, small, *, nseq, seqlen, row0, tm, pos0, want_av, mix_prev):
    rows = z.shape[0]
    gw = GROUP_WIDTH
    tiles = seqlen // tm
    block0 = row0 // tm
    assert tiles * tm == seqlen and block0 * tm == row0
    tile = lambda i, j: (block0 + i * tiles + j, 0)
    seq = lambda i, j: (layer, i, 0, 0)
    par3 = lambda i, j: (layer, 0, 0)
    par4 = lambda i, j: (layer, 0, 0, 0)
    in_specs = [
        pl.BlockSpec((tm, D_IN), tile),
        pl.BlockSpec((None, 1, B_KERNEL - 1, gw), seq),
        pl.BlockSpec((None, 1, C_KERNEL - 1, gw), seq),
        pl.BlockSpec((None, 1, POOL_HIST, gw), seq),
        pl.BlockSpec((None, N_HEADS, A_CHUNK, A_CHUNK), par4),
        pl.BlockSpec((None, N_HEADS, A_CHUNK, HEAD_DIM), par4),
        pl.BlockSpec((None, B_KERNEL, gw), par3),
        pl.BlockSpec((None, C_KERNEL, gw), par3),
        pl.BlockSpec((None, 1, gw), par3),
        pl.BlockSpec((None, 1, gw), par3),
        pl.BlockSpec((None, 1, gw), par3),
        pl.BlockSpec((None, len(POOL_WINDOWS), POOL_GROUP, POOL_GROUP), par4),
        pl.BlockSpec((None, 1, gw), par3),
    ]
    args = [z, sb, sc, sd, *small]
    aliases = {}
    if mix_prev is not None:
        aliases[len(args)] = 0
        in_specs.append(pl.BlockSpec(memory_space=pl.ANY))
        args.append(mix_prev)
    hist = lambda i, j: (i, 0, 0)
    out_specs = [
        pl.BlockSpec((tm, D_MODEL), tile),
        pl.BlockSpec((1, B_KERNEL - 1, gw), hist),
        pl.BlockSpec((1, C_KERNEL - 1, gw), hist),
        pl.BlockSpec((1, POOL_HIST, gw), hist),
    ]
    out_shape = [
        jax.ShapeDtypeStruct((rows, D_MODEL), jnp.bfloat16),
        jax.ShapeDtypeStruct((nseq, B_KERNEL - 1, gw), jnp.float32),
        jax.ShapeDtypeStruct((nseq, C_KERNEL - 1, gw), jnp.float32),
        jax.ShapeDtypeStruct((nseq, POOL_HIST, gw), jnp.float32),
    ]
    if want_av:
        out_specs.append(pl.BlockSpec((1, tm, gw), lambda i, j: (i, j, 0)))
        out_shape.append(jax.ShapeDtypeStruct((nseq, seqlen, gw), jnp.float32))
    return pl.pallas_call(
        functools.partial(_mixer_kernel, tm=tm, pos0=pos0, want_av=want_av, has_alias=mix_prev is not None),
        grid=(nseq, tiles),
        in_specs=in_specs,
        out_specs=out_specs,
        out_shape=out_shape,
        input_output_aliases=aliases,
        scratch_shapes=[
            pltpu.VMEM((N_HEADS, ROW_STRIDE * (B_PAD + tm), LANES), jnp.float32),
            pltpu.VMEM((N_HEADS, ROW_STRIDE * (C_PAD + tm), LANES), jnp.float32),
            pltpu.VMEM((N_HEADS, ROW_STRIDE * (D_PAD + tm), LANES), jnp.float32),
        ],
        compiler_params=_cparams(("parallel", "arbitrary")),
        name="mixers",
    )(*args)


def kernel(x_prompt, x_sample, state_conv_b, state_conv_c, state_pool, g_mix, w_in, a_ws, a_b, b_conv, c_conv,
           c_conv_b, c_ln_g, c_ln_b, d_w, d_scale, w_out, g_ffn, w_ff1, w_ff2, g_final):
    depth = g_mix.shape[0]
    bp, tp, d = x_prompt.shape
    bs, ts, _ = x_sample.shape
    rows_p, rows_s = bp * tp, bs * ts
    gw = GROUP_WIDTH
    ab_rows = jnp.broadcast_to(a_b[:, :, :, None], (depth, N_HEADS, A_CHUNK, HEAD_DIM))
    row = lambda p: p.reshape(depth, 1, gw)
    small = (a_ws, ab_rows, b_conv, c_conv, row(c_conv_b), row(c_ln_g), row(c_ln_b), d_w, row(d_scale))
    zb = jnp.zeros((depth, bp, B_KERNEL - 1, gw), x_prompt.dtype)
    zc = jnp.zeros((depth, bp, C_KERNEL - 1, gw), x_prompt.dtype)
    zd = jnp.zeros((depth, bp, POOL_HIST, gw), x_prompt.dtype)

    x, h = _gather_norm([x_prompt.reshape(rows_p, d), x_sample.reshape(rows_s, d)], g_mix[0])
    new_p, new_s, av_s = [], [], []
    for l in range(depth):
        if l > 0:
            h = _rmsnorm(x, g_mix[l], jnp.bfloat16)
        z = _matmul(h, w_in, l, out_dtype=jnp.float32, name="w_in", **TILES["w_in"])
        outs_p = _mixers(z, zb, zc, zd, l, small, nseq=bp, seqlen=tp, row0=0, tm=A_CHUNK, pos0=0, want_av=False,
                         mix_prev=None)
        outs_s = _mixers(z, state_conv_b, state_conv_c, state_pool, l, small, nseq=bs, seqlen=ts, row0=rows_p,
                         tm=ts, pos0=PAST_LEN, want_av=True, mix_prev=outs_p[0])
        new_p.append(outs_p[1:4])
        new_s.append(outs_s[1:4])
        av_s.append(outs_s[4])
        x, xg, ssq = _matmul(outs_s[0], w_out, l, out_dtype=jnp.float32, res=x, norm_g=g_ffn[l], name="w_out",
                             **TILES["w_out"])
        u = _matmul(xg, w_ff1, l, out_dtype=jnp.bfloat16, relu2=True, ssq=ssq, name="w_ff1", **TILES["w_ff1"])
        x = _matmul(u, w_ff2, l, out_dtype=jnp.float32, res=x, name="w_ff2", **TILES["w_ff2"])
    y_p = _rmsnorm(x, g_final, jnp.float32, block0=0, nblocks=rows_p // NORM_ROWS).reshape(bp, tp, d)
    y_s = _rmsnorm(x, g_final, jnp.float32, block0=rows_p // NORM_ROWS, nblocks=rows_s // NORM_ROWS).reshape(bs, ts, d)
    stack = lambda per_layer, which: jnp.stack([o[which] for o in per_layer])
    return (y_p, y_s, stack(new_p, 0), stack(new_p, 1), stack(new_p, 2), stack(new_s, 0), stack(new_s, 1),
            stack(new_s, 2), jnp.stack(av_s))
```

```python
import functools
import math

import jax
import jax.numpy as jnp
from jax import lax
from jax.experimental import pallas as pl
from jax.experimental.pallas import tpu as pltpu

D_MODEL = 4096
GROUP_WIDTH = D_MODEL // 4
D_IN = 8 * GROUP_WIDTH
D_FF = 4 * D_MODEL
A_CHUNK = 128
HEAD_DIM = 128
N_HEADS = GROUP_WIDTH // HEAD_DIM
B_KERNEL = 3
C_KERNEL = 31
POOL_WINDOWS = (2, 4, 8, 16)
POOL_GROUP = GROUP_WIDTH // len(POOL_WINDOWS)
POOL_HIST = max(POOL_WINDOWS) - 1
PAST_LEN = 2048
EPS = 1e-6

SUBLANES = 8
LANES = 128
VMEM_LIMIT_BYTES = 62 * 1024 * 1024
N_ROW_TILES = 8
NORM_ROWS = 512
EPILOGUE_COLS = 512

ROW_STRIDE = 2
B_PAD = SUBLANES
C_PAD = 4 * SUBLANES
D_PAD = 2 * SUBLANES


def _cparams(semantics):
    return pltpu.CompilerParams(dimension_semantics=semantics, vmem_limit_bytes=VMEM_LIMIT_BYTES)


def _rmsnorm_kernel(x_ref, g_ref, o_ref):
    x = x_ref[...]
    ms = jnp.mean(x * x, axis=-1, keepdims=True)
    o_ref[...] = (x * lax.rsqrt(ms + EPS) * g_ref[...]).astype(o_ref.dtype)


def _rmsnorm(x, g, out_dtype, *, block0=0, nblocks=None):
    m, d = x.shape
    nblocks = m // NORM_ROWS if nblocks is None else nblocks
    return pl.pallas_call(
        _rmsnorm_kernel,
        grid=(nblocks,),
        in_specs=[pl.BlockSpec((NORM_ROWS, d), lambda i: (i + block0, 0)), pl.BlockSpec((1, d), lambda i: (0, 0))],
        out_specs=pl.BlockSpec((NORM_ROWS, d), lambda i: (i, 0)),
        out_shape=jax.ShapeDtypeStruct((nblocks * NORM_ROWS, d), out_dtype),
        compiler_params=_cparams(("parallel",)),
        name="rmsnorm",
    )(x, g.reshape(1, d))


def _gather_norm_kernel(*refs):
    x_ref, g_ref = refs[0], refs[1]
    x_out_ref, h_ref = refs[-2], refs[-1]
    x = x_ref[...]
    x_out_ref[...] = x
    ms = jnp.mean(x * x, axis=-1, keepdims=True)
    h_ref[...] = (x * lax.rsqrt(ms + EPS) * g_ref[...]).astype(h_ref.dtype)


def _gather_norm(parts, g):
    d = parts[0].shape[1]
    rows = sum(p.shape[0] for p in parts)
    outs, block0 = None, 0
    for p in parts:
        nblocks = p.shape[0] // NORM_ROWS
        assert nblocks * NORM_ROWS == p.shape[0]
        in_specs = [pl.BlockSpec((NORM_ROWS, d), lambda i: (i, 0)), pl.BlockSpec((1, d), lambda i: (0, 0))]
        args = [p, g.reshape(1, d)]
        aliases = {}
        if outs is not None:
            aliases = {2: 0, 3: 1}
            in_specs += [pl.BlockSpec(memory_space=pl.ANY)] * 2
            args += list(outs)
        out_spec = pl.BlockSpec((NORM_ROWS, d), lambda i, block0=block0: (i + block0, 0))
        outs = pl.pallas_call(
            _gather_norm_kernel,
            grid=(nblocks,),
            in_specs=in_specs,
            out_specs=[out_spec, out_spec],
            out_shape=[jax.ShapeDtypeStruct((rows, d), jnp.float32), jax.ShapeDtypeStruct((rows, d), jnp.bfloat16)],
            input_output_aliases=aliases,
            compiler_params=_cparams(("parallel",)),
            name="gather_norm",
        )(*args)
        block0 += nblocks
    return outs


def _matmul_kernel(*refs, nk, kdim, relu2, has_res, emit_bf16, scale_rows, emit_norm, n_alias):
    it = iter(refs)
    x_ref, w_ref = next(it), next(it)
    res_ref = next(it) if has_res else None
    ssq_in_ref = next(it) if scale_rows else None
    g_ref = next(it) if emit_norm else None
    for _ in range(n_alias):
        next(it)
    o_ref = next(it)
    wb_ref = next(it) if emit_bf16 else None
    xg_ref, ssq_ref = (next(it), next(it)) if emit_norm else (None, None)
    j, k = pl.program_id(1), pl.program_id(2)

    if emit_bf16:
        wb_ref[...] = w_ref[...].astype(jnp.bfloat16)
        w_ref = wb_ref

    bn = o_ref.shape[1]
    chunks = [slice(c, c + EPILOGUE_COLS) for c in range(0, bn, EPILOGUE_COLS)]

    def product(cs):
        return jnp.dot(x_ref[...], w_ref[:, cs], preferred_element_type=jnp.float32)

    def emit(v, cs, lanes):
        xg_ref[:, cs] = (v * g_ref[:, cs]).astype(xg_ref.dtype)
        sq = v * v
        for c in range(sq.shape[1] // LANES):
            piece = sq[:, c * LANES:(c + 1) * LANES]
            lanes = piece if lanes is None else lanes + piece
        return lanes

    if emit_norm:
        @pl.when((j == 0) & (k == 0))
        def _():
            ssq_ref[...] = jnp.zeros_like(ssq_ref)

    if nk == 1:
        if scale_rows:
            ms = jnp.sum(ssq_in_ref[...], axis=-1, keepdims=True) * (1.0 / kdim)
            scale = lax.rsqrt(ms + EPS)
        lanes = None
        for cs in chunks:
            part = product(cs)
            if scale_rows:
                part = part * scale
            if relu2:
                r = jnp.maximum(part, 0.0)
                part = r * r
            if has_res:
                part = res_ref[:, cs] + part
            o_ref[:, cs] = part.astype(o_ref.dtype)
            if emit_norm:
                lanes = emit(part, cs, lanes)
        if emit_norm:
            ssq_ref[...] += lanes
    else:
        assert o_ref.dtype == jnp.float32 and not scale_rows and not relu2

        @pl.when(k == 0)
        def _():
            for cs in chunks:
                o_ref[:, cs] = (res_ref[:, cs] + product(cs)) if has_res else product(cs)

        @pl.when(k > 0)
        def _():
            for cs in chunks:
                o_ref[:, cs] = o_ref[:, cs] + product(cs)

        if emit_norm:
            @pl.when(k == nk - 1)
            def _():
                lanes = None
                for cs in chunks:
                    lanes = emit(o_ref[:, cs], cs, lanes)
                ssq_ref[...] += lanes


def _matmul_call(x, w, *, layer, tile0, ntiles, prev, bm, bn, bk, out_dtype, relu2, res, ssq, norm_g, name):
    m, kdim = x.shape
    n = w.shape[-1]
    nk = kdim // bk
    emit_bf16 = layer is not None
    rows = lambda i, j, k: (i + tile0, j)
    if emit_bf16:
        assert ntiles == 1 and w.dtype == jnp.float32
        w_spec = pl.BlockSpec((None, bk, bn), lambda i, j, k: (layer, k, j))
    else:
        assert w.dtype == jnp.bfloat16
        w_spec = pl.BlockSpec((bk, bn), lambda i, j, k: (k, j))
    in_specs = [pl.BlockSpec((bm, bk), lambda i, j, k: (i + tile0, k)), w_spec]
    args = [x, w]
    if res is not None:
        in_specs.append(pl.BlockSpec((bm, bn), rows))
        args.append(res)
    if ssq is not None:
        in_specs.append(pl.BlockSpec((bm, LANES), lambda i, j, k: (i + tile0, 0)))
        args.append(ssq)
    if norm_g is not None:
        in_specs.append(pl.BlockSpec((1, bn), lambda i, j, k: (0, j)))
        args.append(norm_g.reshape(1, n))
    out_specs = [pl.BlockSpec((bm, bn), rows)]
    out_shape = [jax.ShapeDtypeStruct((m, n), out_dtype)]
    if emit_bf16:
        out_specs.append(pl.BlockSpec((bk, bn), lambda i, j, k: (k, j)))
        out_shape.append(jax.ShapeDtypeStruct((kdim, n), jnp.bfloat16))
    if norm_g is not None:
        out_specs += [pl.BlockSpec((bm, bn), rows), pl.BlockSpec((bm, LANES), lambda i, j, k: (i + tile0, 0))]
        out_shape += [jax.ShapeDtypeStruct((m, n), jnp.bfloat16), jax.ShapeDtypeStruct((m, LANES), jnp.float32)]
    aliases = {}
    prev = [] if prev is None else prev
    for out_index, arr in enumerate(prev):
        aliases[len(args)] = out_index
        in_specs.append(pl.BlockSpec(memory_space=pl.ANY))
        args.append(arr)
    return pl.pallas_call(
        functools.partial(_matmul_kernel, nk=nk, kdim=kdim, relu2=relu2, has_res=res is not None,
                          emit_bf16=emit_bf16, scale_rows=ssq is not None, emit_norm=norm_g is not None,
                          n_alias=len(prev)),
        grid=(ntiles, n // bn, nk),
        in_specs=in_specs,
        out_specs=out_specs,
        out_shape=out_shape,
        input_output_aliases=aliases,
        compiler_params=_cparams(("parallel", "arbitrary", "arbitrary")),
        name=name,
    )(*args)


def _matmul(x, w_stack, layer, *, first, rest, out_dtype, relu2=False, res=None, ssq=None, norm_g=None, name):
    m = x.shape[0]
    bm = m // N_ROW_TILES
    assert bm * N_ROW_TILES == m and bm % (2 * SUBLANES) == 0
    common = dict(bm=bm, out_dtype=out_dtype, relu2=relu2, res=res, ssq=ssq, norm_g=norm_g)
    outs = _matmul_call(x, w_stack, layer=layer, tile0=0, ntiles=1, prev=None, bn=first[0], bk=first[1],
                        name=name + "_first", **common)
    wb = outs[1]
    keep = [outs[0]] + list(outs[2:])
    outs = _matmul_call(x, wb, layer=None, tile0=1, ntiles=N_ROW_TILES - 1, prev=keep, bn=rest[0], bk=rest[1],
                        name=name, **common)
    return outs[0] if norm_g is None else tuple(outs)


TILES = {
    "w_in": dict(first=(512, D_MODEL), rest=(1024, D_MODEL)),
    "w_out": dict(first=(512, D_MODEL), rest=(1024, D_MODEL)),
    "w_ff1": dict(first=(512, D_MODEL), rest=(2048, D_MODEL)),
    "w_ff2": dict(first=(1024, D_MODEL // 2), rest=(1024, D_MODEL)),
}


def _gelu(x):
    return 0.5 * x * (1.0 + lax.erf(x * math.sqrt(0.5)))


def _sigmoid(x):
    return 1.0 / (1.0 + jnp.exp(-x))


def _rows(first, count):
    return pl.ds(ROW_STRIDE * first, count, stride=ROW_STRIDE)


def _mixer_kernel(*refs, tm, pos0, want_av, has_alias):
    (z_ref, sb_ref, sc_ref, sd_ref, ws_ref, ab_ref, bconv_ref, cconv_ref, ccb_ref, lng_ref, lnb_ref, dw_ref,
     dscale_ref) = refs[:13]
    outs = refs[13 + has_alias:]
    mix_ref, nb_ref, nc_ref, nd_ref = outs[:4]
    av_ref = outs[4] if want_av else None
    bufb, bufc, bufd = outs[4 + want_av:]
    t = pl.program_id(1)
    gw = GROUP_WIDTH
    hb, hc, hd = B_KERNEL - 1, C_KERNEL - 1, POOL_HIST

    def zcols(group, h):
        return z_ref[:, group * gw + h * HEAD_DIM:group * gw + (h + 1) * HEAD_DIM]

    @pl.when(t == 0)
    def _():
        for h in range(N_HEADS):
            cs = slice(h * HEAD_DIM, (h + 1) * HEAD_DIM)
            bufb[h, _rows(B_PAD - hb, hb), :] = sb_ref[0, :, cs]
            bufc[h, _rows(C_PAD - hc, hc), :] = sc_ref[0, :, cs]
            bufd[h, _rows(D_PAD - hd, hd), :] = sd_ref[0, :, cs]

    nch = max(tm // A_CHUNK, 1)
    cr = min(tm, A_CHUNK)
    tri = lax.broadcasted_iota(jnp.int32, (cr, cr), 0) >= lax.broadcasted_iota(jnp.int32, (cr, cr), 1)
    pos = pos0 + t * tm + lax.broadcasted_iota(jnp.int32, (tm, HEAD_DIM), 0)

    pooled_minus_token = []
    for h in range(N_HEADS):
        cs = slice(h * HEAD_DIM, (h + 1) * HEAD_DIM)

        wm = jnp.where(tri, ws_ref[h, 0:cr, 0:cr], 0.0).astype(jnp.bfloat16)
        bias = ab_ref[h, 0:cr, :]
        for c in range(nch):
            rs = slice(c * cr, (c + 1) * cr)
            u = _gelu(z_ref[rs, h * HEAD_DIM:(h + 1) * HEAD_DIM])
            v = _gelu(z_ref[rs, gw + h * HEAD_DIM:gw + (h + 1) * HEAD_DIM])
            if av_ref is not None:
                av_ref[0, rs, cs] = v
            mixv = jnp.dot(wm, v.astype(jnp.bfloat16), preferred_element_type=jnp.float32) + bias
            mix_ref[rs, cs] = (u * mixv).astype(mix_ref.dtype)

        bufb[h, _rows(B_PAD, tm), :] = zcols(4, h) * zcols(2, h)
        conv_b = bconv_ref[0:1, cs] * bufb[h, _rows(B_PAD - hb, tm), :]
        for k in range(1, B_KERNEL):
            conv_b = conv_b + bconv_ref[k:k + 1, cs] * bufb[h, _rows(B_PAD - hb + k, tm), :]
        mix_ref[:, gw + h * HEAD_DIM:gw + (h + 1) * HEAD_DIM] = (zcols(3, h) * conv_b).astype(mix_ref.dtype)

        bufc[h, _rows(C_PAD, tm), :] = zcols(5, h) * _sigmoid(zcols(6, h))
        conv_c = cconv_ref[0:1, cs] * bufc[h, _rows(C_PAD - hc, tm), :]
        for k in range(1, C_KERNEL):
            conv_c = conv_c + cconv_ref[k:k + 1, cs] * bufc[h, _rows(C_PAD - hc + k, tm), :]
        cf = conv_c + ccb_ref[0:1, cs]
        mu = jnp.mean(cf, axis=-1, keepdims=True)
        cen = cf - mu
        var = jnp.mean(cen * cen, axis=-1, keepdims=True)
        a = cen * lax.rsqrt(var + EPS) * lng_ref[0:1, cs] + lnb_ref[0:1, cs]
        mix_ref[:, 2 * gw + h * HEAD_DIM:2 * gw + (h + 1) * HEAD_DIM] = (a * _sigmoid(a)).astype(mix_ref.dtype)

        w = POOL_WINDOWS[h * HEAD_DIM // POOL_GROUP]
        p = zcols(7, h)
        bufd[h, _rows(D_PAD, tm), :] = p
        s = p
        for i in range(1, w):
            s = s + bufd[h, _rows(D_PAD - i, tm), :]
        cnt = jnp.minimum(pos + 1, w).astype(jnp.float32)
        pooled_minus_token.append((s / cnt - p).astype(jnp.bfloat16))

        nb = bufb[h, _rows(tm + B_PAD - hb, hb), :]
        nc = bufc[h, _rows(tm + C_PAD - hc, hc), :]
        nd = bufd[h, _rows(tm + D_PAD - hd, hd), :]
        nb_ref[0, :, cs] = nb
        nc_ref[0, :, cs] = nc
        nd_ref[0, :, cs] = nd
        bufb[h, _rows(B_PAD - hb, hb), :] = nb
        bufc[h, _rows(C_PAD - hc, hc), :] = nc
        bufd[h, _rows(D_PAD - hd, hd), :] = nd

    slabs = POOL_GROUP // HEAD_DIM
    for g in range(len(POOL_WINDOWS)):
        gs = slice(g * POOL_GROUP, (g + 1) * POOL_GROUP)
        d = jnp.concatenate(pooled_minus_token[g * slabs:(g + 1) * slabs], axis=-1)
        yd = jnp.dot(d, dw_ref[g].astype(jnp.bfloat16), preferred_element_type=jnp.float32) * dscale_ref[0:1, gs]
        mix_ref[:, 3 * gw + g * POOL_GROUP:3 * gw + (g + 1) * POOL_GROUP] = yd.astype(mix_ref.dtype)


def _mixers(z, sb, sc, sd, layer, small, *, nseq, seqlen, row0, tm, pos0, want_av, mix_prev):
    rows = z.shape[0]
    gw = GROUP_WIDTH
    tiles = seqlen // tm
    block0 = row0 // tm
    assert tiles * tm == seqlen and block0 * tm == row0
    tile = lambda i, j: (block0 + i * tiles + j, 0)
    seq = lambda i, j: (layer, i, 0, 0)
    par3 = lambda i, j: (layer, 0, 0)
    par4 = lambda i, j: (layer, 0, 0, 0)
    in_specs = [
        pl.BlockSpec((tm, D_IN), tile),
        pl.BlockSpec((None, 1, B_KERNEL - 1, gw), seq),
        pl.BlockSpec((None, 1, C_KERNEL - 1, gw), seq),
        pl.BlockSpec((None, 1, POOL_HIST, gw), seq),
        pl.BlockSpec((None, N_HEADS, A_CHUNK, A_CHUNK), par4),
        pl.BlockSpec((None, N_HEADS, A_CHUNK, HEAD_DIM), par4),
        pl.BlockSpec((None, B_KERNEL, gw), par3),
        pl.BlockSpec((None, C_KERNEL, gw), par3),
        pl.BlockSpec((None, 1, gw), par3),
        pl.BlockSpec((None, 1, gw), par3),
        pl.BlockSpec((None, 1, gw), par3),
        pl.BlockSpec((None, len(POOL_WINDOWS), POOL_GROUP, POOL_GROUP), par4),
        pl.BlockSpec((None, 1, gw), par3),
    ]
    args = [z, sb, sc, sd, *small]
    aliases = {}
    if mix_prev is not None:
        aliases[len(args)] = 0
        in_specs.append(pl.BlockSpec(memory_space=pl.ANY))
        args.append(mix_prev)
    hist = lambda i, j: (i, 0, 0)
    out_specs = [
        pl.BlockSpec((tm, D_MODEL), tile),
        pl.BlockSpec((1, B_KERNEL - 1, gw), hist),
        pl.BlockSpec((1, C_KERNEL - 1, gw), hist),
        pl.BlockSpec((1, POOL_HIST, gw), hist),
    ]
    out_shape = [
        jax.ShapeDtypeStruct((rows, D_MODEL), jnp.bfloat16),
        jax.ShapeDtypeStruct((nseq, B_KERNEL - 1, gw), jnp.float32),
        jax.ShapeDtypeStruct((nseq, C_KERNEL - 1, gw), jnp.float32),
        jax.ShapeDtypeStruct((nseq, POOL_HIST, gw), jnp.float32),
    ]
    if want_av:
        out_specs.append(pl.BlockSpec((1, tm, gw), lambda i, j: (i, j, 0)))
        out_shape.append(jax.ShapeDtypeStruct((nseq, seqlen, gw), jnp.float32))
    return pl.pallas_call(
        functools.partial(_mixer_kernel, tm=tm, pos0=pos0, want_av=want_av, has_alias=mix_prev is not None),
        grid=(nseq, tiles),
        in_specs=in_specs,
        out_specs=out_specs,
        out_shape=out_shape,
        input_output_aliases=aliases,
        scratch_shapes=[
            pltpu.VMEM((N_HEADS, ROW_STRIDE * (B_PAD + tm), LANES), jnp.float32),
            pltpu.VMEM((N_HEADS, ROW_STRIDE * (C_PAD + tm), LANES), jnp.float32),
            pltpu.VMEM((N_HEADS, ROW_STRIDE * (D_PAD + tm), LANES), jnp.float32),
        ],
        compiler_params=_cparams(("parallel", "arbitrary")),
        name="mixers",
    )(*args)


def kernel(x_prompt, x_sample, state_conv_b, state_conv_c, state_pool, g_mix, w_in, a_ws, a_b, b_conv, c_conv,
           c_conv_b, c_ln_g, c_ln_b, d_w, d_scale, w_out, g_ffn, w_ff1, w_ff2, g_final):
    depth = g_mix.shape[0]
    bp, tp, d = x_prompt.shape
    bs, ts, _ = x_sample.shape
    rows_p, rows_s = bp * tp, bs * ts
    gw = GROUP_WIDTH
    ab_rows = jnp.broadcast_to(a_b[:, :, :, None], (depth, N_HEADS, A_CHUNK, HEAD_DIM))
    row = lambda p: p.reshape(depth, 1, gw)
    small = (a_ws, ab_rows, b_conv, c_conv, row(c_conv_b), row(c_ln_g), row(c_ln_b), d_w, row(d_scale))
    zb = jnp.zeros((depth, bp, B_KERNEL - 1, gw), x_prompt.dtype)
    zc = jnp.zeros((depth, bp, C_KERNEL - 1, gw), x_prompt.dtype)
    zd = jnp.zeros((depth, bp, POOL_HIST, gw), x_prompt.dtype)

    x, h = _gather_norm([x_prompt.reshape(rows_p, d), x_sample.reshape(rows_s, d)], g_mix[0])
    new_p, new_s, av_s = [], [], []
    ssq = None
    for l in range(depth):
        z = _matmul(h, w_in, l, out_dtype=jnp.float32, ssq=ssq, name="w_in", **TILES["w_in"])
        outs_p = _mixers(z, zb, zc, zd, l, small, nseq=bp, seqlen=tp, row0=0, tm=A_CHUNK, pos0=0, want_av=False,
                         mix_prev=None)
        outs_s = _mixers(z, state_conv_b, state_conv_c, state_pool, l, small, nseq=bs, seqlen=ts, row0=rows_p,
                         tm=ts, pos0=PAST_LEN, want_av=True, mix_prev=outs_p[0])
        new_p.append(outs_p[1:4])
        new_s.append(outs_s[1:4])
        av_s.append(outs_s[4])
        x, xg, ssq = _matmul(outs_s[0], w_out, l, out_dtype=jnp.float32, res=x, norm_g=g_ffn[l], name="w_out",
                             **TILES["w_out"])
        u = _matmul(xg, w_ff1, l, out_dtype=jnp.bfloat16, relu2=True, ssq=ssq, name="w_ff1", **TILES["w_ff1"])
        if l + 1 < depth:
            x, h, ssq = _matmul(u, w_ff2, l, out_dtype=jnp.float32, res=x, norm_g=g_mix[l + 1], name="w_ff2",
                                **TILES["w_ff2"])
        else:
            x = _matmul(u, w_ff2, l, out_dtype=jnp.float32, res=x, name="w_ff2", **TILES["w_ff2"])
    y_p = _rmsnorm(x, g_final, jnp.float32, block0=0, nblocks=rows_p // NORM_ROWS).reshape(bp, tp, d)
    y_s = _rmsnorm(x, g_final, jnp.float32, block0=rows_p // NORM_ROWS, nblocks=rows_s // NORM_ROWS).reshape(bs, ts, d)
    stack = lambda per_layer, which: jnp.stack([o[which] for o in per_layer])
    return (y_p, y_s, stack(new_p, 0), stack(new_p, 1), stack(new_p, 2), stack(new_s, 0), stack(new_s, 1),
            stack(new_s, 2), jnp.stack(av_s))
```

```python
import functools
import math

import jax
import jax.numpy as jnp
from jax import lax
from jax.experimental import pallas as pl
from jax.experimental.pallas import tpu as pltpu

D_MODEL = 4096
GROUP_WIDTH = D_MODEL // 4
D_IN = 8 * GROUP_WIDTH
D_FF = 4 * D_MODEL
A_CHUNK = 128
HEAD_DIM = 128
N_HEADS = GROUP_WIDTH // HEAD_DIM
B_KERNEL = 3
C_KERNEL = 31
POOL_WINDOWS = (2, 4, 8, 16)
POOL_GROUP = GROUP_WIDTH // len(POOL_WINDOWS)
POOL_HIST = max(POOL_WINDOWS) - 1
PAST_LEN = 2048
EPS = 1e-6

SUBLANES = 8
LANES = 128
VMEM_LIMIT_BYTES = 62 * 1024 * 1024
N_ROW_TILES = 8
NORM_ROWS = 512
EPILOGUE_COLS = 512

ROW_STRIDE = 2
B_PAD = SUBLANES
C_PAD = 4 * SUBLANES
D_PAD = 2 * SUBLANES


def _cparams(semantics):
    return pltpu.CompilerParams(dimension_semantics=semantics, vmem_limit_bytes=VMEM_LIMIT_BYTES)


def _rmsnorm_kernel(x_ref, g_ref, o_ref):
    x = x_ref[...]
    ms = jnp.mean(x * x, axis=-1, keepdims=True)
    o_ref[...] = (x * lax.rsqrt(ms + EPS) * g_ref[...]).astype(o_ref.dtype)


def _rmsnorm(x, g, out_dtype, *, block0=0, nblocks=None):
    m, d = x.shape
    nblocks = m // NORM_ROWS if nblocks is None else nblocks
    return pl.pallas_call(
        _rmsnorm_kernel,
        grid=(nblocks,),
        in_specs=[pl.BlockSpec((NORM_ROWS, d), lambda i: (i + block0, 0)), pl.BlockSpec((1, d), lambda i: (0, 0))],
        out_specs=pl.BlockSpec((NORM_ROWS, d), lambda i: (i, 0)),
        out_shape=jax.ShapeDtypeStruct((nblocks * NORM_ROWS, d), out_dtype),
        compiler_params=_cparams(("parallel",)),
        name="rmsnorm",
    )(x, g.reshape(1, d))


def _gather_norm_kernel(*refs):
    x_ref, g_ref = refs[0], refs[1]
    x_out_ref, h_ref = refs[-2], refs[-1]
    x = x_ref[...]
    x_out_ref[...] = x
    ms = jnp.mean(x * x, axis=-1, keepdims=True)
    h_ref[...] = (x * lax.rsqrt(ms + EPS) * g_ref[...]).astype(h_ref.dtype)


def _gather_norm(parts, g):
    d = parts[0].shape[1]
    rows = sum(p.shape[0] for p in parts)
    outs, block0 = None, 0
    for p in parts:
        nblocks = p.shape[0] // NORM_ROWS
        assert nblocks * NORM_ROWS == p.shape[0]
        in_specs = [pl.BlockSpec((NORM_ROWS, d), lambda i: (i, 0)), pl.BlockSpec((1, d), lambda i: (0, 0))]
        args = [p, g.reshape(1, d)]
        aliases = {}
        if outs is not None:
            aliases = {2: 0, 3: 1}
            in_specs += [pl.BlockSpec(memory_space=pl.ANY)] * 2
            args += list(outs)
        out_spec = pl.BlockSpec((NORM_ROWS, d), lambda i, block0=block0: (i + block0, 0))
        outs = pl.pallas_call(
            _gather_norm_kernel,
            grid=(nblocks,),
            in_specs=in_specs,
            out_specs=[out_spec, out_spec],
            out_shape=[jax.ShapeDtypeStruct((rows, d), jnp.float32), jax.ShapeDtypeStruct((rows, d), jnp.bfloat16)],
            input_output_aliases=aliases,
            compiler_params=_cparams(("parallel",)),
            name="gather_norm",
        )(*args)
        block0 += nblocks
    return outs


def _matmul_kernel(*refs, nk, kdim, relu2, has_res, emit_bf16, scale_rows, emit_norm, n_alias):
    it = iter(refs)
    x_ref, w_ref = next(it), next(it)
    res_ref = next(it) if has_res else None
    ssq_in_ref = next(it) if scale_rows else None
    g_ref = next(it) if emit_norm else None
    for _ in range(n_alias):
        next(it)
    o_ref = next(it)
    wb_ref = next(it) if emit_bf16 else None
    xg_ref, ssq_ref = (next(it), next(it)) if emit_norm else (None, None)
    j, k = pl.program_id(1), pl.program_id(2)

    if emit_bf16:
        wb_ref[...] = w_ref[...].astype(jnp.bfloat16)
        w_ref = wb_ref

    bn = o_ref.shape[1]
    chunks = [slice(c, min(c + EPILOGUE_COLS, bn)) for c in range(0, bn, EPILOGUE_COLS)]

    def product(cs):
        return jnp.dot(x_ref[...], w_ref[:, cs], preferred_element_type=jnp.float32)

    def emit(v, cs, lanes):
        xg_ref[:, cs] = (v * g_ref[:, cs]).astype(xg_ref.dtype)
        sq = v * v
        for c in range(sq.shape[1] // LANES):
            piece = sq[:, c * LANES:(c + 1) * LANES]
            lanes = piece if lanes is None else lanes + piece
        return lanes

    if emit_norm:
        @pl.when((j == 0) & (k == 0))
        def _():
            ssq_ref[...] = jnp.zeros_like(ssq_ref)

    if nk == 1:
        if scale_rows:
            ms = jnp.sum(ssq_in_ref[...], axis=-1, keepdims=True) * (1.0 / kdim)
            scale = lax.rsqrt(ms + EPS)
        lanes = None
        for cs in chunks:
            part = product(cs)
            if scale_rows:
                part = part * scale
            if relu2:
                r = jnp.maximum(part, 0.0)
                part = r * r
            if has_res:
                part = res_ref[:, cs] + part
            o_ref[:, cs] = part.astype(o_ref.dtype)
            if emit_norm:
                lanes = emit(part, cs, lanes)
        if emit_norm:
            ssq_ref[...] += lanes
    else:
        assert o_ref.dtype == jnp.float32 and not scale_rows and not relu2

        @pl.when(k == 0)
        def _():
            for cs in chunks:
                o_ref[:, cs] = (res_ref[:, cs] + product(cs)) if has_res else product(cs)

        @pl.when(k > 0)
        def _():
            for cs in chunks:
                o_ref[:, cs] = o_ref[:, cs] + product(cs)

        if emit_norm:
            @pl.when(k == nk - 1)
            def _():
                lanes = None
                for cs in chunks:
                    lanes = emit(o_ref[:, cs], cs, lanes)
                ssq_ref[...] += lanes


def _matmul_call(x, w, *, layer, tile0, ntiles, prev, bm, bn, bk, out_dtype, relu2, res, ssq, norm_g, name):
    m, kdim = x.shape
    n = w.shape[-1]
    nk = kdim // bk
    emit_bf16 = layer is not None
    rows = lambda i, j, k: (i + tile0, j)
    if emit_bf16:
        assert ntiles == 1 and w.dtype == jnp.float32
        w_spec = pl.BlockSpec((None, bk, bn), lambda i, j, k: (layer, k, j))
    else:
        assert w.dtype == jnp.bfloat16
        w_spec = pl.BlockSpec((bk, bn), lambda i, j, k: (k, j))
    x_mode = dict(pipeline_mode=pl.Buffered(1)) if ntiles == 1 and nk == 1 else {}
    in_specs = [pl.BlockSpec((bm, bk), lambda i, j, k: (i + tile0, k), **x_mode), w_spec]
    args = [x, w]
    if res is not None:
        in_specs.append(pl.BlockSpec((bm, bn), rows))
        args.append(res)
    if ssq is not None:
        in_specs.append(pl.BlockSpec((bm, LANES), lambda i, j, k: (i + tile0, 0)))
        args.append(ssq)
    if norm_g is not None:
        in_specs.append(pl.BlockSpec((1, bn), lambda i, j, k: (0, j)))
        args.append(norm_g.reshape(1, n))
    out_specs = [pl.BlockSpec((bm, bn), rows)]
    out_shape = [jax.ShapeDtypeStruct((m, n), out_dtype)]
    if emit_bf16:
        out_specs.append(pl.BlockSpec((bk, bn), lambda i, j, k: (k, j)))
        out_shape.append(jax.ShapeDtypeStruct((kdim, n), jnp.bfloat16))
    if norm_g is not None:
        out_specs += [pl.BlockSpec((bm, bn), rows), pl.BlockSpec((bm, LANES), lambda i, j, k: (i + tile0, 0))]
        out_shape += [jax.ShapeDtypeStruct((m, n), jnp.bfloat16), jax.ShapeDtypeStruct((m, LANES), jnp.float32)]
    aliases = {}
    prev = [] if prev is None else prev
    for out_index, arr in enumerate(prev):
        aliases[len(args)] = out_index
        in_specs.append(pl.BlockSpec(memory_space=pl.ANY))
        args.append(arr)
    return pl.pallas_call(
        functools.partial(_matmul_kernel, nk=nk, kdim=kdim, relu2=relu2, has_res=res is not None,
                          emit_bf16=emit_bf16, scale_rows=ssq is not None, emit_norm=norm_g is not None,
                          n_alias=len(prev)),
        grid=(ntiles, n // bn, nk),
        in_specs=in_specs,
        out_specs=out_specs,
        out_shape=out_shape,
        input_output_aliases=aliases,
        compiler_params=_cparams(("parallel", "arbitrary", "arbitrary")),
        name=name,
    )(*args)


def _matmul(x, w_stack, layer, *, first, rest, first_tiles, out_dtype, relu2=False, res=None, ssq=None, norm_g=None,
            name):
    m = x.shape[0]
    bm = m // N_ROW_TILES
    assert bm * N_ROW_TILES == m and bm % (2 * SUBLANES) == 0
    common = dict(out_dtype=out_dtype, relu2=relu2, res=res, ssq=ssq, norm_g=norm_g)
    outs = _matmul_call(x, w_stack, layer=layer, tile0=0, ntiles=1, prev=None, bm=first_tiles * bm, bn=first[0],
                        bk=first[1], name=name + "_first", **common)
    wb = outs[1]
    keep = [outs[0]] + list(outs[2:])
    outs = _matmul_call(x, wb, layer=None, tile0=first_tiles, ntiles=N_ROW_TILES - first_tiles, prev=keep, bm=bm,
                        bn=rest[0], bk=rest[1], name=name, **common)
    return outs[0] if norm_g is None else tuple(outs)


TILES = {
    "w_in": dict(first=(512, D_MODEL), rest=(1024, D_MODEL), first_tiles=2),
    "w_out": dict(first=(256, D_MODEL), rest=(1024, D_MODEL), first_tiles=2),
    "w_ff1": dict(first=(512, D_MODEL), rest=(2048, D_MODEL), first_tiles=2),
    "w_ff2": dict(first=(1024, D_MODEL // 2), rest=(1024, D_MODEL), first_tiles=1),
}


def _gelu(x):
    return 0.5 * x * (1.0 + lax.erf(x * math.sqrt(0.5)))


def _sigmoid(x):
    return 1.0 / (1.0 + jnp.exp(-x))


def _rows(first, count):
    return pl.ds(ROW_STRIDE * first, count, stride=ROW_STRIDE)


def _mixer_kernel(*refs, tm, pos0, want_av, has_alias):
    (z_ref, sb_ref, sc_ref, sd_ref, ws_ref, ab_ref, bconv_ref, cconv_ref, ccb_ref, lng_ref, lnb_ref, dw_ref,
     dscale_ref) = refs[:13]
    outs = refs[13 + has_alias:]
    mix_ref, nb_ref, nc_ref, nd_ref = outs[:4]
    av_ref = outs[4] if want_av else None
    bufb, bufc, bufd = outs[4 + want_av:]
    t = pl.program_id(1)
    gw = GROUP_WIDTH
    hb, hc, hd = B_KERNEL - 1, C_KERNEL - 1, POOL_HIST

    def zcols(group, h):
        return z_ref[:, group * gw + h * HEAD_DIM:group * gw + (h + 1) * HEAD_DIM]

    @pl.when(t == 0)
    def _():
        for h in range(N_HEADS):
            cs = slice(h * HEAD_DIM, (h + 1) * HEAD_DIM)
            bufb[h, _rows(B_PAD - hb, hb), :] = sb_ref[0, :, cs]
            bufc[h, _rows(C_PAD - hc, hc), :] = sc_ref[0, :, cs]
            bufd[h, _rows(D_PAD - hd, hd), :] = sd_ref[0, :, cs]

    nch = max(tm // A_CHUNK, 1)
    cr = min(tm, A_CHUNK)
    tri = lax.broadcasted_iota(jnp.int32, (cr, cr), 0) >= lax.broadcasted_iota(jnp.int32, (cr, cr), 1)
    pos = pos0 + t * tm + lax.broadcasted_iota(jnp.int32, (tm, HEAD_DIM), 0)

    pooled_minus_token = []
    for h in range(N_HEADS):
        cs = slice(h * HEAD_DIM, (h + 1) * HEAD_DIM)

        wm = jnp.where(tri, ws_ref[h, 0:cr, 0:cr], 0.0).astype(jnp.bfloat16)
        bias = ab_ref[h, 0:cr, :]
        for c in range(nch):
            rs = slice(c * cr, (c + 1) * cr)
            u = _gelu(z_ref[rs, h * HEAD_DIM:(h + 1) * HEAD_DIM])
            v = _gelu(z_ref[rs, gw + h * HEAD_DIM:gw + (h + 1) * HEAD_DIM])
            if av_ref is not None:
                av_ref[0, rs, cs] = v
            mixv = jnp.dot(wm, v.astype(jnp.bfloat16), preferred_element_type=jnp.float32) + bias
            mix_ref[rs, cs] = (u * mixv).astype(mix_ref.dtype)

        bufb[h, _rows(B_PAD, tm), :] = zcols(4, h) * zcols(2, h)
        conv_b = bconv_ref[0:1, cs] * bufb[h, _rows(B_PAD - hb, tm), :]
        for k in range(1, B_KERNEL):
            conv_b = conv_b + bconv_ref[k:k + 1, cs] * bufb[h, _rows(B_PAD - hb + k, tm), :]
        mix_ref[:, gw + h * HEAD_DIM:gw + (h + 1) * HEAD_DIM] = (zcols(3, h) * conv_b).astype(mix_ref.dtype)

        bufc[h, _rows(C_PAD, tm), :] = zcols(5, h) * _sigmoid(zcols(6, h))
        conv_c = cconv_ref[0:1, cs] * bufc[h, _rows(C_PAD - hc, tm), :]
        for k in range(1, C_KERNEL):
            conv_c = conv_c + cconv_ref[k:k + 1, cs] * bufc[h, _rows(C_PAD - hc + k, tm), :]
        cf = conv_c + ccb_ref[0:1, cs]
        mu = jnp.mean(cf, axis=-1, keepdims=True)
        cen = cf - mu
        var = jnp.mean(cen * cen, axis=-1, keepdims=True)
        a = cen * lax.rsqrt(var + EPS) * lng_ref[0:1, cs] + lnb_ref[0:1, cs]
        mix_ref[:, 2 * gw + h * HEAD_DIM:2 * gw + (h + 1) * HEAD_DIM] = (a * _sigmoid(a)).astype(mix_ref.dtype)

        w = POOL_WINDOWS[h * HEAD_DIM // POOL_GROUP]
        p = zcols(7, h)
        bufd[h, _rows(D_PAD, tm), :] = p
        s = p
        for i in range(1, w):
            s = s + bufd[h, _rows(D_PAD - i, tm), :]
        cnt = jnp.minimum(pos + 1, w).astype(jnp.float32)
        pooled_minus_token.append((s / cnt - p).astype(jnp.bfloat16))

        nb = bufb[h, _rows(tm + B_PAD - hb, hb), :]
        nc = bufc[h, _rows(tm + C_PAD - hc, hc), :]
        nd = bufd[h, _rows(tm + D_PAD - hd, hd), :]
        nb_ref[0, :, cs] = nb
        nc_ref[0, :, cs] = nc
        nd_ref[0, :, cs] = nd
        bufb[h, _rows(B_PAD - hb, hb), :] = nb
        bufc[h, _rows(C_PAD - hc, hc), :] = nc
        bufd[h, _rows(D_PAD - hd, hd), :] = nd

    slabs = POOL_GROUP // HEAD_DIM
    for g in range(len(POOL_WINDOWS)):
        gs = slice(g * POOL_GROUP, (g + 1) * POOL_GROUP)
        d = jnp.concatenate(pooled_minus_token[g * slabs:(g + 1) * slabs], axis=-1)
        yd = jnp.dot(d, dw_ref[g].astype(jnp.bfloat16), preferred_element_type=jnp.float32) * dscale_ref[0:1, gs]
        mix_ref[:, 3 * gw + g * POOL_GROUP:3 * gw + (g + 1) * POOL_GROUP] = yd.astype(mix_ref.dtype)


def _mixers(z, sb, sc, sd, layer, small, *, nseq, seqlen, row0, tm, pos0, want_av, mix_prev):
    rows = z.shape[0]
    gw = GROUP_WIDTH
    tiles = seqlen // tm
    block0 = row0 // tm
    assert tiles * tm == seqlen and block0 * tm == row0
    tile = lambda i, j: (block0 + i * tiles + j, 0)
    seq = lambda i, j: (layer, i, 0, 0)
    par3 = lambda i, j: (layer, 0, 0)
    par4 = lambda i, j: (layer, 0, 0, 0)
    in_specs = [
        pl.BlockSpec((tm, D_IN), tile),
        pl.BlockSpec((None, 1, B_KERNEL - 1, gw), seq),
        pl.BlockSpec((None, 1, C_KERNEL - 1, gw), seq),
        pl.BlockSpec((None, 1, POOL_HIST, gw), seq),
        pl.BlockSpec((None, N_HEADS, A_CHUNK, A_CHUNK), par4),
        pl.BlockSpec((None, N_HEADS, A_CHUNK, HEAD_DIM), par4),
        pl.BlockSpec((None, B_KERNEL, gw), par3),
        pl.BlockSpec((None, C_KERNEL, gw), par3),
        pl.BlockSpec((None, 1, gw), par3),
        pl.BlockSpec((None, 1, gw), par3),
        pl.BlockSpec((None, 1, gw), par3),
        pl.BlockSpec((None, len(POOL_WINDOWS), POOL_GROUP, POOL_GROUP), par4),
        pl.BlockSpec((None, 1, gw), par3),
    ]
    args = [z, sb, sc, sd, *small]
    aliases = {}
    if mix_prev is not None:
        aliases[len(args)] = 0
        in_specs.append(pl.BlockSpec(memory_space=pl.ANY))
        args.append(mix_prev)
    hist = lambda i, j: (i, 0, 0)
    out_specs = [
        pl.BlockSpec((tm, D_MODEL), tile),
        pl.BlockSpec((1, B_KERNEL - 1, gw), hist),
        pl.BlockSpec((1, C_KERNEL - 1, gw), hist),
        pl.BlockSpec((1, POOL_HIST, gw), hist),
    ]
    out_shape = [
        jax.ShapeDtypeStruct((rows, D_MODEL), jnp.bfloat16),
        jax.ShapeDtypeStruct((nseq, B_KERNEL - 1, gw), jnp.float32),
        jax.ShapeDtypeStruct((nseq, C_KERNEL - 1, gw), jnp.float32),
        jax.ShapeDtypeStruct((nseq, POOL_HIST, gw), jnp.float32),
    ]
    if want_av:
        out_specs.append(pl.BlockSpec((1, tm, gw), lambda i, j: (i, j, 0)))
        out_shape.append(jax.ShapeDtypeStruct((nseq, seqlen, gw), jnp.float32))
    return pl.pallas_call(
        functools.partial(_mixer_kernel, tm=tm, pos0=pos0, want_av=want_av, has_alias=mix_prev is not None),
        grid=(nseq, tiles),
        in_specs=in_specs,
        out_specs=out_specs,
        out_shape=out_shape,
        input_output_aliases=aliases,
        scratch_shapes=[
            pltpu.VMEM((N_HEADS, ROW_STRIDE * (B_PAD + tm), LANES), jnp.float32),
            pltpu.VMEM((N_HEADS, ROW_STRIDE * (C_PAD + tm), LANES), jnp.float32),
            pltpu.VMEM((N_HEADS, ROW_STRIDE * (D_PAD + tm), LANES), jnp.float32),
        ],
        compiler_params=_cparams(("parallel", "arbitrary")),
        name="mixers",
    )(*args)


def kernel(x_prompt, x_sample, state_conv_b, state_conv_c, state_pool, g_mix, w_in, a_ws, a_b, b_conv, c_conv,
           c_conv_b, c_ln_g, c_ln_b, d_w, d_scale, w_out, g_ffn, w_ff1, w_ff2, g_final):
    depth = g_mix.shape[0]
    bp, tp, d = x_prompt.shape
    bs, ts, _ = x_sample.shape
    rows_p, rows_s = bp * tp, bs * ts
    gw = GROUP_WIDTH
    ab_rows = jnp.broadcast_to(a_b[:, :, :, None], (depth, N_HEADS, A_CHUNK, HEAD_DIM))
    row = lambda p: p.reshape(depth, 1, gw)
    small = (a_ws, ab_rows, b_conv, c_conv, row(c_conv_b), row(c_ln_g), row(c_ln_b), d_w, row(d_scale))
    zb = jnp.zeros((depth, bp, B_KERNEL - 1, gw), x_prompt.dtype)
    zc = jnp.zeros((depth, bp, C_KERNEL - 1, gw), x_prompt.dtype)
    zd = jnp.zeros((depth, bp, POOL_HIST, gw), x_prompt.dtype)

    x, h = _gather_norm([x_prompt.reshape(rows_p, d), x_sample.reshape(rows_s, d)], g_mix[0])
    new_p, new_s, av_s = [], [], []
    ssq = None
    for l in range(depth):
        z = _matmul(h, w_in, l, out_dtype=jnp.float32, ssq=ssq, name="w_in", **TILES["w_in"])
        outs_p = _mixers(z, zb, zc, zd, l, small, nseq=bp, seqlen=tp, row0=0, tm=A_CHUNK, pos0=0, want_av=False,
                         mix_prev=None)
        outs_s = _mixers(z, state_conv_b, state_conv_c, state_pool, l, small, nseq=bs, seqlen=ts, row0=rows_p,
                         tm=ts, pos0=PAST_LEN, want_av=True, mix_prev=outs_p[0])
        new_p.append(outs_p[1:4])
        new_s.append(outs_s[1:4])
        av_s.append(outs_s[4])
        x, xg, ssq = _matmul(outs_s[0], w_out, l, out_dtype=jnp.float32, res=x, norm_g=g_ffn[l], name="w_out",
                             **TILES["w_out"])
        u = _matmul(xg, w_ff1, l, out_dtype=jnp.bfloat16, relu2=True, ssq=ssq, name="w_ff1", **TILES["w_ff1"])
        if l + 1 < depth:
            x, h, ssq = _matmul(u, w_ff2, l, out_dtype=jnp.float32, res=x, norm_g=g_mix[l + 1], name="w_ff2",
                                **TILES["w_ff2"])
        else:
            x = _matmul(u, w_ff2, l, out_dtype=jnp.float32, res=x, name="w_ff2", **TILES["w_ff2"])
    y_p = _rmsnorm(x, g_final, jnp.float32, block0=0, nblocks=rows_p // NORM_ROWS).reshape(bp, tp, d)
    y_s = _rmsnorm(x, g_final, jnp.float32, block0=rows_p // NORM_ROWS, nblocks=rows_s // NORM_ROWS).reshape(bs, ts, d)
    stack = lambda per_layer, which: jnp.stack([o[which] for o in per_layer])
    return (y_p, y_s, stack(new_p, 0), stack(new_p, 1), stack(new_p, 2), stack(new_s, 0), stack(new_s, 1),
            stack(new_s, 2), jnp.stack(av_s))
```

```python
import functools
import math

import jax
import jax.numpy as jnp
from jax import lax
from jax.experimental import pallas as pl
from jax.experimental.pallas import tpu as pltpu

D_MODEL = 4096
GROUP_WIDTH = D_MODEL // 4
D_IN = 8 * GROUP_WIDTH
D_FF = 4 * D_MODEL
A_CHUNK = 128
HEAD_DIM = 128
N_HEADS = GROUP_WIDTH // HEAD_DIM
B_KERNEL = 3
C_KERNEL = 31
POOL_WINDOWS = (2, 4, 8, 16)
POOL_GROUP = GROUP_WIDTH // len(POOL_WINDOWS)
POOL_HIST = max(POOL_WINDOWS) - 1
PAST_LEN = 2048
EPS = 1e-6

SUBLANES = 8
LANES = 128
VMEM_LIMIT_BYTES = 62 * 1024 * 1024
N_ROW_TILES = 8
NORM_ROWS = 512
EPILOGUE_COLS = 512

ROW_STRIDE = 2
B_PAD = SUBLANES
C_PAD = 4 * SUBLANES
D_PAD = 2 * SUBLANES


def _cparams(semantics):
    return pltpu.CompilerParams(dimension_semantics=semantics, vmem_limit_bytes=VMEM_LIMIT_BYTES)


def _rmsnorm_kernel(x_ref, g_ref, o_ref):
    x = x_ref[...]
    ms = jnp.mean(x * x, axis=-1, keepdims=True)
    o_ref[...] = (x * lax.rsqrt(ms + EPS) * g_ref[...]).astype(o_ref.dtype)


def _rmsnorm(x, g, out_dtype, *, block0=0, nblocks=None):
    m, d = x.shape
    nblocks = m // NORM_ROWS if nblocks is None else nblocks
    return pl.pallas_call(
        _rmsnorm_kernel,
        grid=(nblocks,),
        in_specs=[pl.BlockSpec((NORM_ROWS, d), lambda i: (i + block0, 0)), pl.BlockSpec((1, d), lambda i: (0, 0))],
        out_specs=pl.BlockSpec((NORM_ROWS, d), lambda i: (i, 0)),
        out_shape=jax.ShapeDtypeStruct((nblocks * NORM_ROWS, d), out_dtype),
        compiler_params=_cparams(("parallel",)),
        name="rmsnorm",
    )(x, g.reshape(1, d))


def _gather_norm_kernel(*refs):
    x_ref, g_ref = refs[0], refs[1]
    x_out_ref, h_ref = refs[-2], refs[-1]
    x = x_ref[...]
    x_out_ref[...] = x
    ms = jnp.mean(x * x, axis=-1, keepdims=True)
    h_ref[...] = (x * lax.rsqrt(ms + EPS) * g_ref[...]).astype(h_ref.dtype)


def _gather_norm(parts, g):
    d = parts[0].shape[1]
    rows = sum(p.shape[0] for p in parts)
    outs, block0 = None, 0
    for p in parts:
        nblocks = p.shape[0] // NORM_ROWS
        assert nblocks * NORM_ROWS == p.shape[0]
        in_specs = [pl.BlockSpec((NORM_ROWS, d), lambda i: (i, 0)), pl.BlockSpec((1, d), lambda i: (0, 0))]
        args = [p, g.reshape(1, d)]
        aliases = {}
        if outs is not None:
            aliases = {2: 0, 3: 1}
            in_specs += [pl.BlockSpec(memory_space=pl.ANY)] * 2
            args += list(outs)
        out_spec = pl.BlockSpec((NORM_ROWS, d), lambda i, block0=block0: (i + block0, 0))
        outs = pl.pallas_call(
            _gather_norm_kernel,
            grid=(nblocks,),
            in_specs=in_specs,
            out_specs=[out_spec, out_spec],
            out_shape=[jax.ShapeDtypeStruct((rows, d), jnp.float32), jax.ShapeDtypeStruct((rows, d), jnp.bfloat16)],
            input_output_aliases=aliases,
            compiler_params=_cparams(("parallel",)),
            name="gather_norm",
        )(*args)
        block0 += nblocks
    return outs


def _matmul_kernel(*refs, nk, kdim, relu2, has_res, emit_bf16, scale_rows, emit_norm, n_alias, ring_layer):
    it = iter(refs)
    x_ref, w_ref = next(it), next(it)
    res_ref = next(it) if has_res else None
    ssq_in_ref = next(it) if scale_rows else None
    g_ref = next(it) if emit_norm else None
    for _ in range(n_alias):
        next(it)
    o_ref = next(it)
    wb_ref = next(it) if emit_bf16 else None
    xg_ref, ssq_ref = (next(it), next(it)) if emit_norm else (None, None)
    j, k = pl.program_id(1), pl.program_id(2)

    if ring_layer is not None:
        wbuf, sem = next(it), next(it)
        ring, bk, bn_w = wbuf.shape
        nsteps = pl.num_programs(1) * nk
        s = j * nk + k

        def weight_copy(step):
            slot = lax.rem(step, ring)
            rows = pl.ds(pl.multiple_of(lax.rem(step, nk) * bk, bk), bk)
            cols = pl.ds(pl.multiple_of((step // nk) * bn_w, bn_w), bn_w)
            return pltpu.make_async_copy(w_ref.at[ring_layer, rows, cols], wbuf.at[slot], sem.at[slot])

        @pl.when(s == 0)
        def _():
            for ahead in range(ring - 1):
                weight_copy(ahead).start()

        @pl.when(s + (ring - 1) < nsteps)
        def _():
            weight_copy(s + (ring - 1)).start()

        weight_copy(s).wait()
        w_ref = wbuf.at[lax.rem(s, ring)]

    if emit_bf16:
        wb_ref[...] = w_ref[...].astype(jnp.bfloat16)
        w_ref = wb_ref

    bn = o_ref.shape[1]
    chunks = [slice(c, c + EPILOGUE_COLS) for c in range(0, bn, EPILOGUE_COLS)]

    def product(cs):
        return jnp.dot(x_ref[...], w_ref[:, cs], preferred_element_type=jnp.float32)

    def emit(v, cs, lanes):
        xg_ref[:, cs] = (v * g_ref[:, cs]).astype(xg_ref.dtype)
        sq = v * v
        for c in range(sq.shape[1] // LANES):
            piece = sq[:, c * LANES:(c + 1) * LANES]
            lanes = piece if lanes is None else lanes + piece
        return lanes

    if emit_norm:
        @pl.when((j == 0) & (k == 0))
        def _():
            ssq_ref[...] = jnp.zeros_like(ssq_ref)

    if nk == 1:
        if scale_rows:
            ms = jnp.sum(ssq_in_ref[...], axis=-1, keepdims=True) * (1.0 / kdim)
            scale = lax.rsqrt(ms + EPS)
        lanes = None
        for cs in chunks:
            part = product(cs)
            if scale_rows:
                part = part * scale
            if relu2:
                r = jnp.maximum(part, 0.0)
                part = r * r
            if has_res:
                part = res_ref[:, cs] + part
            o_ref[:, cs] = part.astype(o_ref.dtype)
            if emit_norm:
                lanes = emit(part, cs, lanes)
        if emit_norm:
            ssq_ref[...] += lanes
    else:
        assert o_ref.dtype == jnp.float32 and not scale_rows and not relu2

        @pl.when(k == 0)
        def _():
            for cs in chunks:
                o_ref[:, cs] = (res_ref[:, cs] + product(cs)) if has_res else product(cs)

        @pl.when(k > 0)
        def _():
            for cs in chunks:
                o_ref[:, cs] = o_ref[:, cs] + product(cs)

        if emit_norm:
            @pl.when(k == nk - 1)
            def _():
                lanes = None
                for cs in chunks:
                    lanes = emit(o_ref[:, cs], cs, lanes)
                ssq_ref[...] += lanes


def _matmul_call(x, w, *, layer, tile0, ntiles, prev, bm, bn, bk, out_dtype, relu2, res, ssq, norm_g, name,
                 ring=None):
    m, kdim = x.shape
    n = w.shape[-1]
    nk = kdim // bk
    emit_bf16 = layer is not None
    rows = lambda i, j, k: (i + tile0, j)
    scratch = []
    if emit_bf16:
        assert ntiles == 1 and w.dtype == jnp.float32 and (n // bn) * nk >= ring >= 2
        w_spec = pl.BlockSpec(memory_space=pl.ANY)
        scratch = [pltpu.VMEM((ring, bk, bn), jnp.float32), pltpu.SemaphoreType.DMA((ring,))]
    else:
        assert w.dtype == jnp.bfloat16
        w_spec = pl.BlockSpec((bk, bn), lambda i, j, k: (k, j))
    in_specs = [pl.BlockSpec((bm, bk), lambda i, j, k: (i + tile0, k)), w_spec]
    args = [x, w]
    if res is not None:
        in_specs.append(pl.BlockSpec((bm, bn), rows))
        args.append(res)
    if ssq is not None:
        in_specs.append(pl.BlockSpec((bm, LANES), lambda i, j, k: (i + tile0, 0)))
        args.append(ssq)
    if norm_g is not None:
        in_specs.append(pl.BlockSpec((1, bn), lambda i, j, k: (0, j)))
        args.append(norm_g.reshape(1, n))
    out_specs = [pl.BlockSpec((bm, bn), rows)]
    out_shape = [jax.ShapeDtypeStruct((m, n), out_dtype)]
    if emit_bf16:
        out_specs.append(pl.BlockSpec((bk, bn), lambda i, j, k: (k, j)))
        out_shape.append(jax.ShapeDtypeStruct((kdim, n), jnp.bfloat16))
    if norm_g is not None:
        out_specs += [pl.BlockSpec((bm, bn), rows), pl.BlockSpec((bm, LANES), lambda i, j, k: (i + tile0, 0))]
        out_shape += [jax.ShapeDtypeStruct((m, n), jnp.bfloat16), jax.ShapeDtypeStruct((m, LANES), jnp.float32)]
    aliases = {}
    prev = [] if prev is None else prev
    for out_index, arr in enumerate(prev):
        aliases[len(args)] = out_index
        in_specs.append(pl.BlockSpec(memory_space=pl.ANY))
        args.append(arr)
    return pl.pallas_call(
        functools.partial(_matmul_kernel, nk=nk, kdim=kdim, relu2=relu2, has_res=res is not None,
                          emit_bf16=emit_bf16, scale_rows=ssq is not None, emit_norm=norm_g is not None,
                          n_alias=len(prev), ring_layer=layer),
        grid=(ntiles, n // bn, nk),
        in_specs=in_specs,
        out_specs=out_specs,
        out_shape=out_shape,
        scratch_shapes=scratch,
        input_output_aliases=aliases,
        compiler_params=_cparams(("arbitrary" if emit_bf16 else "parallel", "arbitrary", "arbitrary")),
        name=name,
    )(*args)


def _matmul(x, w_stack, layer, *, first, rest, ring, out_dtype, relu2=False, res=None, ssq=None, norm_g=None, name):
    m = x.shape[0]
    bm = m // N_ROW_TILES
    assert bm * N_ROW_TILES == m and bm % (2 * SUBLANES) == 0
    common = dict(bm=bm, out_dtype=out_dtype, relu2=relu2, res=res, ssq=ssq, norm_g=norm_g)
    outs = _matmul_call(x, w_stack, layer=layer, tile0=0, ntiles=1, prev=None, bn=first[0], bk=first[1],
                        name=name + "_first", ring=ring, **common)
    wb = outs[1]
    keep = [outs[0]] + list(outs[2:])
    outs = _matmul_call(x, wb, layer=None, tile0=1, ntiles=N_ROW_TILES - 1, prev=keep, bn=rest[0], bk=rest[1],
                        name=name, **common)
    return outs[0] if norm_g is None else tuple(outs)


TILES = {
    "w_in": dict(first=(512, D_MODEL), rest=(1024, D_MODEL), ring=3),
    "w_out": dict(first=(512, D_MODEL), rest=(1024, D_MODEL), ring=3),
    "w_ff1": dict(first=(512, D_MODEL), rest=(2048, D_MODEL), ring=3),
    "w_ff2": dict(first=(1024, D_MODEL // 2), rest=(1024, D_MODEL), ring=2),
}


def _gelu(x):
    return 0.5 * x * (1.0 + lax.erf(x * math.sqrt(0.5)))


def _sigmoid(x):
    return 1.0 / (1.0 + jnp.exp(-x))


def _rows(first, count):
    return pl.ds(ROW_STRIDE * first, count, stride=ROW_STRIDE)


def _mixer_kernel(*refs, tm, pos0, want_av, has_alias):
    (z_ref, sb_ref, sc_ref, sd_ref, ws_ref, ab_ref, bconv_ref, cconv_ref, ccb_ref, lng_ref, lnb_ref, dw_ref,
     dscale_ref) = refs[:13]
    outs = refs[13 + has_alias:]
    mix_ref, nb_ref, nc_ref, nd_ref = outs[:4]
    av_ref = outs[4] if want_av else None
    bufb, bufc, bufd = outs[4 + want_av:]
    t = pl.program_id(1)
    gw = GROUP_WIDTH
    hb, hc, hd = B_KERNEL - 1, C_KERNEL - 1, POOL_HIST

    def zcols(group, h):
        return z_ref[:, group * gw + h * HEAD_DIM:group * gw + (h + 1) * HEAD_DIM]

    @pl.when(t == 0)
    def _():
        for h in range(N_HEADS):
            cs = slice(h * HEAD_DIM, (h + 1) * HEAD_DIM)
            bufb[h, _rows(B_PAD - hb, hb), :] = sb_ref[0, :, cs]
            bufc[h, _rows(C_PAD - hc, hc), :] = sc_ref[0, :, cs]
            bufd[h, _rows(D_PAD - hd, hd), :] = sd_ref[0, :, cs]

    nch = max(tm // A_CHUNK, 1)
    cr = min(tm, A_CHUNK)
    tri = lax.broadcasted_iota(jnp.int32, (cr, cr), 0) >= lax.broadcasted_iota(jnp.int32, (cr, cr), 1)
    pos = pos0 + t * tm + lax.broadcasted_iota(jnp.int32, (tm, HEAD_DIM), 0)

    pooled_minus_token = []
    for h in range(N_HEADS):
        cs = slice(h * HEAD_DIM, (h + 1) * HEAD_DIM)

        wm = jnp.where(tri, ws_ref[h, 0:cr, 0:cr], 0.0).astype(jnp.bfloat16)
        bias = ab_ref[h, 0:cr, :]
        for c in range(nch):
            rs = slice(c * cr, (c + 1) * cr)
            u = _gelu(z_ref[rs, h * HEAD_DIM:(h + 1) * HEAD_DIM])
            v = _gelu(z_ref[rs, gw + h * HEAD_DIM:gw + (h + 1) * HEAD_DIM])
            if av_ref is not None:
                av_ref[0, rs, cs] = v
            mixv = jnp.dot(wm, v.astype(jnp.bfloat16), preferred_element_type=jnp.float32) + bias
            mix_ref[rs, cs] = (u * mixv).astype(mix_ref.dtype)

        bufb[h, _rows(B_PAD, tm), :] = zcols(4, h) * zcols(2, h)
        conv_b = bconv_ref[0:1, cs] * bufb[h, _rows(B_PAD - hb, tm), :]
        for k in range(1, B_KERNEL):
            conv_b = conv_b + bconv_ref[k:k + 1, cs] * bufb[h, _rows(B_PAD - hb + k, tm), :]
        mix_ref[:, gw + h * HEAD_DIM:gw + (h + 1) * HEAD_DIM] = (zcols(3, h) * conv_b).astype(mix_ref.dtype)

        bufc[h, _rows(C_PAD, tm), :] = zcols(5, h) * _sigmoid(zcols(6, h))
        conv_c = cconv_ref[0:1, cs] * bufc[h, _rows(C_PAD - hc, tm), :]
        for k in range(1, C_KERNEL):
            conv_c = conv_c + cconv_ref[k:k + 1, cs] * bufc[h, _rows(C_PAD - hc + k, tm), :]
        cf = conv_c + ccb_ref[0:1, cs]
        mu = jnp.mean(cf, axis=-1, keepdims=True)
        cen = cf - mu
        var = jnp.mean(cen * cen, axis=-1, keepdims=True)
        a = cen * lax.rsqrt(var + EPS) * lng_ref[0:1, cs] + lnb_ref[0:1, cs]
        mix_ref[:, 2 * gw + h * HEAD_DIM:2 * gw + (h + 1) * HEAD_DIM] = (a * _sigmoid(a)).astype(mix_ref.dtype)

        w = POOL_WINDOWS[h * HEAD_DIM // POOL_GROUP]
        p = zcols(7, h)
        bufd[h, _rows(D_PAD, tm), :] = p
        s = p
        for i in range(1, w):
            s = s + bufd[h, _rows(D_PAD - i, tm), :]
        cnt = jnp.minimum(pos + 1, w).astype(jnp.float32)
        pooled_minus_token.append((s / cnt - p).astype(jnp.bfloat16))

        nb = bufb[h, _rows(tm + B_PAD - hb, hb), :]
        nc = bufc[h, _rows(tm + C_PAD - hc, hc), :]
        nd = bufd[h, _rows(tm + D_PAD - hd, hd), :]
        nb_ref[0, :, cs] = nb
        nc_ref[0, :, cs] = nc
        nd_ref[0, :, cs] = nd
        bufb[h, _rows(B_PAD - hb, hb), :] = nb
        bufc[h, _rows(C_PAD - hc, hc), :] = nc
        bufd[h, _rows(D_PAD - hd, hd), :] = nd

    slabs = POOL_GROUP // HEAD_DIM
    for g in range(len(POOL_WINDOWS)):
        gs = slice(g * POOL_GROUP, (g + 1) * POOL_GROUP)
        d = jnp.concatenate(pooled_minus_token[g * slabs:(g + 1) * slabs], axis=-1)
        yd = jnp.dot(d, dw_ref[g].astype(jnp.bfloat16), preferred_element_type=jnp.float32) * dscale_ref[0:1, gs]
        mix_ref[:, 3 * gw + g * POOL_GROUP:3 * gw + (g + 1) * POOL_GROUP] = yd.astype(mix_ref.dtype)


def _mixers(z, sb, sc, sd, layer, small, *, nseq, seqlen, row0, tm, pos0, want_av, mix_prev):
    rows = z.shape[0]
    gw = GROUP_WIDTH
    tiles = seqlen // tm
    block0 = row0 // tm
    assert tiles * tm == seqlen and block0 * tm == row0
    tile = lambda i, j: (block0 + i * tiles + j, 0)
    seq = lambda i, j: (layer, i, 0, 0)
    par3 = lambda i, j: (layer, 0, 0)
    par4 = lambda i, j: (layer, 0, 0, 0)
    in_specs = [
        pl.BlockSpec((tm, D_IN), tile),
        pl.BlockSpec((None, 1, B_KERNEL - 1, gw), seq),
        pl.BlockSpec((None, 1, C_KERNEL - 1, gw), seq),
        pl.BlockSpec((None, 1, POOL_HIST, gw), seq),
        pl.BlockSpec((None, N_HEADS, A_CHUNK, A_CHUNK), par4),
        pl.BlockSpec((None, N_HEADS, A_CHUNK, HEAD_DIM), par4),
        pl.BlockSpec((None, B_KERNEL, gw), par3),
        pl.BlockSpec((None, C_KERNEL, gw), par3),
        pl.BlockSpec((None, 1, gw), par3),
        pl.BlockSpec((None, 1, gw), par3),
        pl.BlockSpec((None, 1, gw), par3),
        pl.BlockSpec((None, len(POOL_WINDOWS), POOL_GROUP, POOL_GROUP), par4),
        pl.BlockSpec((None, 1, gw), par3),
    ]
    args = [z, sb, sc, sd, *small]
    aliases = {}
    if mix_prev is not None:
        aliases[len(args)] = 0
        in_specs.append(pl.BlockSpec(memory_space=pl.ANY))
        args.append(mix_prev)
    hist = lambda i, j: (i, 0, 0)
    out_specs = [
        pl.BlockSpec((tm, D_MODEL), tile),
        pl.BlockSpec((1, B_KERNEL - 1, gw), hist),
        pl.BlockSpec((1, C_KERNEL - 1, gw), hist),
        pl.BlockSpec((1, POOL_HIST, gw), hist),
    ]
    out_shape = [
        jax.ShapeDtypeStruct((rows, D_MODEL), jnp.bfloat16),
        jax.ShapeDtypeStruct((nseq, B_KERNEL - 1, gw), jnp.float32),
        jax.ShapeDtypeStruct((nseq, C_KERNEL - 1, gw), jnp.float32),
        jax.ShapeDtypeStruct((nseq, POOL_HIST, gw), jnp.float32),
    ]
    if want_av:
        out_specs.append(pl.BlockSpec((1, tm, gw), lambda i, j: (i, j, 0)))
        out_shape.append(jax.ShapeDtypeStruct((nseq, seqlen, gw), jnp.float32))
    return pl.pallas_call(
        functools.partial(_mixer_kernel, tm=tm, pos0=pos0, want_av=want_av, has_alias=mix_prev is not None),
        grid=(nseq, tiles),
        in_specs=in_specs,
        out_specs=out_specs,
        out_shape=out_shape,
        input_output_aliases=aliases,
        scratch_shapes=[
            pltpu.VMEM((N_HEADS, ROW_STRIDE * (B_PAD + tm), LANES), jnp.float32),
            pltpu.VMEM((N_HEADS, ROW_STRIDE * (C_PAD + tm), LANES), jnp.float32),
            pltpu.VMEM((N_HEADS, ROW_STRIDE * (D_PAD + tm), LANES), jnp.float32),
        ],
        compiler_params=_cparams(("parallel", "arbitrary")),
        name="mixers",
    )(*args)


def kernel(x_prompt, x_sample, state_conv_b, state_conv_c, state_pool, g_mix, w_in, a_ws, a_b, b_conv, c_conv,
           c_conv_b, c_ln_g, c_ln_b, d_w, d_scale, w_out, g_ffn, w_ff1, w_ff2, g_final):
    depth = g_mix.shape[0]
    bp, tp, d = x_prompt.shape
    bs, ts, _ = x_sample.shape
    rows_p, rows_s = bp * tp, bs * ts
    gw = GROUP_WIDTH
    ab_rows = jnp.broadcast_to(a_b[:, :, :, None], (depth, N_HEADS, A_CHUNK, HEAD_DIM))
    row = lambda p: p.reshape(depth, 1, gw)
    small = (a_ws, ab_rows, b_conv, c_conv, row(c_conv_b), row(c_ln_g), row(c_ln_b), d_w, row(d_scale))
    zb = jnp.zeros((depth, bp, B_KERNEL - 1, gw), x_prompt.dtype)
    zc = jnp.zeros((depth, bp, C_KERNEL - 1, gw), x_prompt.dtype)
    zd = jnp.zeros((depth, bp, POOL_HIST, gw), x_prompt.dtype)

    x, h = _gather_norm([x_prompt.reshape(rows_p, d), x_sample.reshape(rows_s, d)], g_mix[0])
    new_p, new_s, av_s = [], [], []
    ssq = None
    for l in range(depth):
        z = _matmul(h, w_in, l, out_dtype=jnp.float32, ssq=ssq, name="w_in", **TILES["w_in"])
        outs_p = _mixers(z, zb, zc, zd, l, small, nseq=bp, seqlen=tp, row0=0, tm=A_CHUNK, pos0=0, want_av=False,
                         mix_prev=None)
        outs_s = _mixers(z, state_conv_b, state_conv_c, state_pool, l, small, nseq=bs, seqlen=ts, row0=rows_p,
                         tm=ts, pos0=PAST_LEN, want_av=True, mix_prev=outs_p[0])
        new_p.append(outs_p[1:4])
        new_s.append(outs_s[1:4])
        av_s.append(outs_s[4])
        x, xg, ssq = _matmul(outs_s[0], w_out, l, out_dtype=jnp.float32, res=x, norm_g=g_ffn[l], name="w_out",
                             **TILES["w_out"])
        u = _matmul(xg, w_ff1, l, out_dtype=jnp.bfloat16, relu2=True, ssq=ssq, name="w_ff1", **TILES["w_ff1"])
        if l + 1 < depth:
            x, h, ssq = _matmul(u, w_ff2, l, out_dtype=jnp.float32, res=x, norm_g=g_mix[l + 1], name="w_ff2",
                                **TILES["w_ff2"])
        else:
            x = _matmul(u, w_ff2, l, out_dtype=jnp.float32, res=x, name="w_ff2", **TILES["w_ff2"])
    y_p = _rmsnorm(x, g_final, jnp.float32, block0=0, nblocks=rows_p // NORM_ROWS).reshape(bp, tp, d)
    y_s = _rmsnorm(x, g_final, jnp.float32, block0=rows_p // NORM_ROWS, nblocks=rows_s // NORM_ROWS).reshape(bs, ts, d)
    stack = lambda per_layer, which: jnp.stack([o[which] for o in per_layer])
    return (y_p, y_s, stack(new_p, 0), stack(new_p, 1), stack(new_p, 2), stack(new_s, 0), stack(new_s, 1),
            stack(new_s, 2), jnp.stack(av_s))
```

```python
import functools
import math

import jax
import jax.numpy as jnp
from jax import lax
from jax.experimental import pallas as pl
from jax.experimental.pallas import tpu as pltpu

D_MODEL = 4096
GROUP_WIDTH = D_MODEL // 4
D_IN = 8 * GROUP_WIDTH
D_FF = 4 * D_MODEL
A_CHUNK = 128
HEAD_DIM = 128
N_HEADS = GROUP_WIDTH // HEAD_DIM
B_KERNEL = 3
C_KERNEL = 31
POOL_WINDOWS = (2, 4, 8, 16)
POOL_GROUP = GROUP_WIDTH // len(POOL_WINDOWS)
POOL_HIST = max(POOL_WINDOWS) - 1
PAST_LEN = 2048
EPS = 1e-6

SUBLANES = 8
LANES = 128
VMEM_LIMIT_BYTES = 62 * 1024 * 1024
N_ROW_TILES = 8
NORM_ROWS = 512
EPILOGUE_COLS = 512

ROW_STRIDE = 2
B_PAD = SUBLANES
C_PAD = 4 * SUBLANES
D_PAD = 2 * SUBLANES


def _cparams(semantics):
    return pltpu.CompilerParams(dimension_semantics=semantics, vmem_limit_bytes=VMEM_LIMIT_BYTES)


def _rmsnorm_kernel(x_ref, g_ref, o_ref):
    x = x_ref[...]
    ms = jnp.mean(x * x, axis=-1, keepdims=True)
    o_ref[...] = (x * lax.rsqrt(ms + EPS) * g_ref[...]).astype(o_ref.dtype)


def _rmsnorm(x, g, out_dtype, *, block0=0, nblocks=None):
    m, d = x.shape
    nblocks = m // NORM_ROWS if nblocks is None else nblocks
    return pl.pallas_call(
        _rmsnorm_kernel,
        grid=(nblocks,),
        in_specs=[pl.BlockSpec((NORM_ROWS, d), lambda i: (i + block0, 0)), pl.BlockSpec((1, d), lambda i: (0, 0))],
        out_specs=pl.BlockSpec((NORM_ROWS, d), lambda i: (i, 0)),
        out_shape=jax.ShapeDtypeStruct((nblocks * NORM_ROWS, d), out_dtype),
        compiler_params=_cparams(("parallel",)),
        name="rmsnorm",
    )(x, g.reshape(1, d))


def _gather_norm_kernel(*refs):
    x_ref, g_ref = refs[0], refs[1]
    x_out_ref, h_ref = refs[-2], refs[-1]
    x = x_ref[...]
    x_out_ref[...] = x
    ms = jnp.mean(x * x, axis=-1, keepdims=True)
    h_ref[...] = (x * lax.rsqrt(ms + EPS) * g_ref[...]).astype(h_ref.dtype)


def _gather_norm(parts, g):
    d = parts[0].shape[1]
    rows = sum(p.shape[0] for p in parts)
    outs, block0 = None, 0
    for p in parts:
        nblocks = p.shape[0] // NORM_ROWS
        assert nblocks * NORM_ROWS == p.shape[0]
        in_specs = [pl.BlockSpec((NORM_ROWS, d), lambda i: (i, 0)), pl.BlockSpec((1, d), lambda i: (0, 0))]
        args = [p, g.reshape(1, d)]
        aliases = {}
        if outs is not None:
            aliases = {2: 0, 3: 1}
            in_specs += [pl.BlockSpec(memory_space=pl.ANY)] * 2
            args += list(outs)
        out_spec = pl.BlockSpec((NORM_ROWS, d), lambda i, block0=block0: (i + block0, 0))
        outs = pl.pallas_call(
            _gather_norm_kernel,
            grid=(nblocks,),
            in_specs=in_specs,
            out_specs=[out_spec, out_spec],
            out_shape=[jax.ShapeDtypeStruct((rows, d), jnp.float32), jax.ShapeDtypeStruct((rows, d), jnp.bfloat16)],
            input_output_aliases=aliases,
            compiler_params=_cparams(("parallel",)),
            name="gather_norm",
        )(*args)
        block0 += nblocks
    return outs


def _matmul_kernel(*refs, nk, kdim, relu2, has_res, emit_bf16, scale_rows, emit_norm, n_alias, group_acts):
    it = iter(refs)
    x_ref, w_ref = next(it), next(it)
    res_ref = next(it) if has_res else None
    ssq_in_ref = next(it) if scale_rows else None
    g_ref = next(it) if emit_norm else None
    for _ in range(n_alias):
        next(it)
    o_ref = next(it)
    wb_ref = next(it) if emit_bf16 else None
    xg_ref, ssq_ref = (next(it), next(it)) if emit_norm else (None, None)
    j, k = pl.program_id(1), pl.program_id(2)

    if emit_bf16:
        wb_ref[...] = w_ref[...].astype(jnp.bfloat16)
        w_ref = wb_ref

    bn = o_ref.shape[1]
    chunks = [slice(c, c + EPILOGUE_COLS) for c in range(0, bn, EPILOGUE_COLS)]

    def product(cs):
        return jnp.dot(x_ref[...], w_ref[:, cs], preferred_element_type=jnp.float32)

    def emit(v, cs, lanes):
        xg_ref[:, cs] = (v * g_ref[:, cs]).astype(xg_ref.dtype)
        sq = v * v
        for c in range(sq.shape[1] // LANES):
            piece = sq[:, c * LANES:(c + 1) * LANES]
            lanes = piece if lanes is None else lanes + piece
        return lanes

    if emit_norm:
        @pl.when((j == 0) & (k == 0))
        def _():
            ssq_ref[...] = jnp.zeros_like(ssq_ref)

    if nk == 1:
        if scale_rows:
            ms = jnp.sum(ssq_in_ref[...], axis=-1, keepdims=True) * (1.0 / kdim)
            scale = lax.rsqrt(ms + EPS)

        def run(act):
            lanes = None
            for cs in chunks:
                part = product(cs)
                if scale_rows:
                    part = part * scale
                if act is not None:
                    part = act(part)
                if relu2:
                    r = jnp.maximum(part, 0.0)
                    part = r * r
                if has_res:
                    part = res_ref[:, cs] + part
                o_ref[:, cs] = part.astype(o_ref.dtype)
                if emit_norm:
                    lanes = emit(part, cs, lanes)
            if emit_norm:
                ssq_ref[...] += lanes

        if group_acts is None:
            run(None)
        else:
            group = j // (GROUP_WIDTH // bn)
            for act in set(group_acts):
                hit = functools.reduce(jnp.logical_or, [group == g for g, a in enumerate(group_acts) if a is act])
                pl.when(hit)(functools.partial(run, act))
    else:
        assert o_ref.dtype == jnp.float32 and not scale_rows and not relu2

        @pl.when(k == 0)
        def _():
            for cs in chunks:
                o_ref[:, cs] = (res_ref[:, cs] + product(cs)) if has_res else product(cs)

        @pl.when((k > 0) & (k < nk - 1) if emit_norm else k > 0)
        def _():
            for cs in chunks:
                o_ref[:, cs] = o_ref[:, cs] + product(cs)

        if emit_norm:
            @pl.when(k == nk - 1)
            def _():
                lanes = None
                for cs in chunks:
                    total = o_ref[:, cs] + product(cs)
                    o_ref[:, cs] = total
                    lanes = emit(total, cs, lanes)
                ssq_ref[...] += lanes


def _matmul_call(x, w, *, layer, tile0, ntiles, prev, bm, bn, bk, out_dtype, relu2, res, ssq, norm_g, group_acts,
                 name):
    m, kdim = x.shape
    n = w.shape[-1]
    nk = kdim // bk
    emit_bf16 = layer is not None
    rows = lambda i, j, k: (i + tile0, j)
    if emit_bf16:
        assert ntiles == 1 and w.dtype == jnp.float32
        w_spec = pl.BlockSpec((None, bk, bn), lambda i, j, k: (layer, k, j))
    else:
        assert w.dtype == jnp.bfloat16
        w_spec = pl.BlockSpec((bk, bn), lambda i, j, k: (k, j))
    in_specs = [pl.BlockSpec((bm, bk), lambda i, j, k: (i + tile0, k)), w_spec]
    args = [x, w]
    if res is not None:
        in_specs.append(pl.BlockSpec((bm, bn), rows))
        args.append(res)
    if ssq is not None:
        in_specs.append(pl.BlockSpec((bm, LANES), lambda i, j, k: (i + tile0, 0)))
        args.append(ssq)
    if norm_g is not None:
        in_specs.append(pl.BlockSpec((1, bn), lambda i, j, k: (0, j)))
        args.append(norm_g.reshape(1, n))
    out_specs = [pl.BlockSpec((bm, bn), rows)]
    out_shape = [jax.ShapeDtypeStruct((m, n), out_dtype)]
    if emit_bf16:
        out_specs.append(pl.BlockSpec((bk, bn), lambda i, j, k: (k, j)))
        out_shape.append(jax.ShapeDtypeStruct((kdim, n), jnp.bfloat16))
    if norm_g is not None:
        out_specs += [pl.BlockSpec((bm, bn), rows), pl.BlockSpec((bm, LANES), lambda i, j, k: (i + tile0, 0))]
        out_shape += [jax.ShapeDtypeStruct((m, n), jnp.bfloat16), jax.ShapeDtypeStruct((m, LANES), jnp.float32)]
    aliases = {}
    prev = [] if prev is None else prev
    for out_index, arr in enumerate(prev):
        aliases[len(args)] = out_index
        in_specs.append(pl.BlockSpec(memory_space=pl.ANY))
        args.append(arr)
    return pl.pallas_call(
        functools.partial(_matmul_kernel, nk=nk, kdim=kdim, relu2=relu2, has_res=res is not None,
                          emit_bf16=emit_bf16, scale_rows=ssq is not None, emit_norm=norm_g is not None,
                          n_alias=len(prev), group_acts=group_acts),
        grid=(ntiles, n // bn, nk),
        in_specs=in_specs,
        out_specs=out_specs,
        out_shape=out_shape,
        input_output_aliases=aliases,
        compiler_params=_cparams(("parallel", "arbitrary", "arbitrary")),
        name=name,
    )(*args)


def _matmul(x, w_stack, layer, *, first, rest, out_dtype, relu2=False, res=None, ssq=None, norm_g=None,
            group_acts=None, name):
    m = x.shape[0]
    bm = m // N_ROW_TILES
    assert bm * N_ROW_TILES == m and bm % (2 * SUBLANES) == 0
    common = dict(bm=bm, out_dtype=out_dtype, relu2=relu2, res=res, ssq=ssq, norm_g=norm_g, group_acts=group_acts)
    outs = _matmul_call(x, w_stack, layer=layer, tile0=0, ntiles=1, prev=None, bn=first[0], bk=first[1],
                        name=name + "_first", **common)
    wb = outs[1]
    keep = [outs[0]] + list(outs[2:])
    outs = _matmul_call(x, wb, layer=None, tile0=1, ntiles=N_ROW_TILES - 1, prev=keep, bn=rest[0], bk=rest[1],
                        name=name, **common)
    return outs[0] if norm_g is None else tuple(outs)


TILES = {
    "w_in": dict(first=(512, D_MODEL), rest=(1024, D_MODEL)),
    "w_out": dict(first=(512, D_MODEL), rest=(1024, D_MODEL)),
    "w_ff1": dict(first=(512, D_MODEL), rest=(2048, D_MODEL)),
    "w_ff2": dict(first=(1024, D_MODEL // 2), rest=(1024, D_MODEL)),
}


def _gelu(x):
    return 0.5 * x * (1.0 + lax.erf(x * math.sqrt(0.5)))


def _sigmoid(x):
    return 1.0 / (1.0 + jnp.exp(-x))


Z_GROUP_ACTS = (_gelu, _gelu, None, None, None, None, _sigmoid, None)


def _rows(first, count):
    return pl.ds(ROW_STRIDE * first, count, stride=ROW_STRIDE)


def _mixer_kernel(*refs, tm, pos0, want_av, has_alias):
    (z_ref, sb_ref, sc_ref, sd_ref, ws_ref, ab_ref, bconv_ref, cconv_ref, ccb_ref, lng_ref, lnb_ref, dw_ref,
     dscale_ref) = refs[:13]
    outs = refs[13 + has_alias:]
    mix_ref, nb_ref, nc_ref, nd_ref = outs[:4]
    av_ref = outs[4] if want_av else None
    bufb, bufc, bufd = outs[4 + want_av:]
    t = pl.program_id(1)
    gw = GROUP_WIDTH
    hb, hc, hd = B_KERNEL - 1, C_KERNEL - 1, POOL_HIST

    def zcols(group, h):
        return z_ref[:, group * gw + h * HEAD_DIM:group * gw + (h + 1) * HEAD_DIM]

    @pl.when(t == 0)
    def _():
        for h in range(N_HEADS):
            cs = slice(h * HEAD_DIM, (h + 1) * HEAD_DIM)
            bufb[h, _rows(B_PAD - hb, hb), :] = sb_ref[0, :, cs]
            bufc[h, _rows(C_PAD - hc, hc), :] = sc_ref[0, :, cs]
            bufd[h, _rows(D_PAD - hd, hd), :] = sd_ref[0, :, cs]

    nch = max(tm // A_CHUNK, 1)
    cr = min(tm, A_CHUNK)
    tri = lax.broadcasted_iota(jnp.int32, (cr, cr), 0) >= lax.broadcasted_iota(jnp.int32, (cr, cr), 1)
    pos = pos0 + t * tm + lax.broadcasted_iota(jnp.int32, (tm, HEAD_DIM), 0)

    pooled_minus_token = []
    for h in range(N_HEADS):
        cs = slice(h * HEAD_DIM, (h + 1) * HEAD_DIM)

        wm = jnp.where(tri, ws_ref[h, 0:cr, 0:cr], 0.0).astype(jnp.bfloat16)
        bias = ab_ref[h, 0:cr, :]
        for c in range(nch):
            rs = slice(c * cr, (c + 1) * cr)
            u = z_ref[rs, h * HEAD_DIM:(h + 1) * HEAD_DIM]
            v = z_ref[rs, gw + h * HEAD_DIM:gw + (h + 1) * HEAD_DIM]
            if av_ref is not None:
                av_ref[0, rs, cs] = v
            mixv = jnp.dot(wm, v.astype(jnp.bfloat16), preferred_element_type=jnp.float32) + bias
            mix_ref[rs, cs] = (u * mixv).astype(mix_ref.dtype)

        bufb[h, _rows(B_PAD, tm), :] = zcols(4, h) * zcols(2, h)
        conv_b = bconv_ref[0:1, cs] * bufb[h, _rows(B_PAD - hb, tm), :]
        for k in range(1, B_KERNEL):
            conv_b = conv_b + bconv_ref[k:k + 1, cs] * bufb[h, _rows(B_PAD - hb + k, tm), :]
        mix_ref[:, gw + h * HEAD_DIM:gw + (h + 1) * HEAD_DIM] = (zcols(3, h) * conv_b).astype(mix_ref.dtype)

        bufc[h, _rows(C_PAD, tm), :] = zcols(5, h) * zcols(6, h)
        conv_c = cconv_ref[0:1, cs] * bufc[h, _rows(C_PAD - hc, tm), :]
        for k in range(1, C_KERNEL):
            conv_c = conv_c + cconv_ref[k:k + 1, cs] * bufc[h, _rows(C_PAD - hc + k, tm), :]
        cf = conv_c + ccb_ref[0:1, cs]
        mu = jnp.mean(cf, axis=-1, keepdims=True)
        cen = cf - mu
        var = jnp.mean(cen * cen, axis=-1, keepdims=True)
        a = cen * lax.rsqrt(var + EPS) * lng_ref[0:1, cs] + lnb_ref[0:1, cs]
        mix_ref[:, 2 * gw + h * HEAD_DIM:2 * gw + (h + 1) * HEAD_DIM] = (a * _sigmoid(a)).astype(mix_ref.dtype)

        w = POOL_WINDOWS[h * HEAD_DIM // POOL_GROUP]
        p = zcols(7, h)
        bufd[h, _rows(D_PAD, tm), :] = p
        s = p
        for i in range(1, w):
            s = s + bufd[h, _rows(D_PAD - i, tm), :]
        cnt = jnp.minimum(pos + 1, w).astype(jnp.float32)
        pooled_minus_token.append((s / cnt - p).astype(jnp.bfloat16))

        nb = bufb[h, _rows(tm + B_PAD - hb, hb), :]
        nc = bufc[h, _rows(tm + C_PAD - hc, hc), :]
        nd = bufd[h, _rows(tm + D_PAD - hd, hd), :]
        nb_ref[0, :, cs] = nb
        nc_ref[0, :, cs] = nc
        nd_ref[0, :, cs] = nd
        bufb[h, _rows(B_PAD - hb, hb), :] = nb
        bufc[h, _rows(C_PAD - hc, hc), :] = nc
        bufd[h, _rows(D_PAD - hd, hd), :] = nd

    slabs = POOL_GROUP // HEAD_DIM
    for g in range(len(POOL_WINDOWS)):
        gs = slice(g * POOL_GROUP, (g + 1) * POOL_GROUP)
        d = jnp.concatenate(pooled_minus_token[g * slabs:(g + 1) * slabs], axis=-1)
        yd = jnp.dot(d, dw_ref[g].astype(jnp.bfloat16), preferred_element_type=jnp.float32) * dscale_ref[0:1, gs]
        mix_ref[:, 3 * gw + g * POOL_GROUP:3 * gw + (g + 1) * POOL_GROUP] = yd.astype(mix_ref.dtype)


def _mixers(z, sb, sc, sd, layer, small, *, nseq, seqlen, row0, tm, pos0, want_av, mix_prev):
    rows = z.shape[0]
    gw = GROUP_WIDTH
    tiles = seqlen // tm
    block0 = row0 // tm
    assert tiles * tm == seqlen and block0 * tm == row0
    tile = lambda i, j: (block0 + i * tiles + j, 0)
    seq = lambda i, j: (layer, i, 0, 0)
    par3 = lambda i, j: (layer, 0, 0)
    par4 = lambda i, j: (layer, 0, 0, 0)
    in_specs = [
        pl.BlockSpec((tm, D_IN), tile),
        pl.BlockSpec((None, 1, B_KERNEL - 1, gw), seq),
        pl.BlockSpec((None, 1, C_KERNEL - 1, gw), seq),
        pl.BlockSpec((None, 1, POOL_HIST, gw), seq),
        pl.BlockSpec((None, N_HEADS, A_CHUNK, A_CHUNK), par4),
        pl.BlockSpec((None, N_HEADS, A_CHUNK, HEAD_DIM), par4),
        pl.BlockSpec((None, B_KERNEL, gw), par3),
        pl.BlockSpec((None, C_KERNEL, gw), par3),
        pl.BlockSpec((None, 1, gw), par3),
        pl.BlockSpec((None, 1, gw), par3),
        pl.BlockSpec((None, 1, gw), par3),
        pl.BlockSpec((None, len(POOL_WINDOWS), POOL_GROUP, POOL_GROUP), par4),
        pl.BlockSpec((None, 1, gw), par3),
    ]
    args = [z, sb, sc, sd, *small]
    aliases = {}
    if mix_prev is not None:
        aliases[len(args)] = 0
        in_specs.append(pl.BlockSpec(memory_space=pl.ANY))
        args.append(mix_prev)
    hist = lambda i, j: (i, 0, 0)
    out_specs = [
        pl.BlockSpec((tm, D_MODEL), tile),
        pl.BlockSpec((1, B_KERNEL - 1, gw), hist),
        pl.BlockSpec((1, C_KERNEL - 1, gw), hist),
        pl.BlockSpec((1, POOL_HIST, gw), hist),
    ]
    out_shape = [
        jax.ShapeDtypeStruct((rows, D_MODEL), jnp.bfloat16),
        jax.ShapeDtypeStruct((nseq, B_KERNEL - 1, gw), jnp.float32),
        jax.ShapeDtypeStruct((nseq, C_KERNEL - 1, gw), jnp.float32),
        jax.ShapeDtypeStruct((nseq, POOL_HIST, gw), jnp.float32),
    ]
    if want_av:
        out_specs.append(pl.BlockSpec((1, tm, gw), lambda i, j: (i, j, 0)))
        out_shape.append(jax.ShapeDtypeStruct((nseq, seqlen, gw), jnp.float32))
    return pl.pallas_call(
        functools.partial(_mixer_kernel, tm=tm, pos0=pos0, want_av=want_av, has_alias=mix_prev is not None),
        grid=(nseq, tiles),
        in_specs=in_specs,
        out_specs=out_specs,
        out_shape=out_shape,
        input_output_aliases=aliases,
        scratch_shapes=[
            pltpu.VMEM((N_HEADS, ROW_STRIDE * (B_PAD + tm), LANES), jnp.float32),
            pltpu.VMEM((N_HEADS, ROW_STRIDE * (C_PAD + tm), LANES), jnp.float32),
            pltpu.VMEM((N_HEADS, ROW_STRIDE * (D_PAD + tm), LANES), jnp.float32),
        ],
        compiler_params=_cparams(("parallel", "arbitrary")),
        name="mixers",
    )(*args)


def kernel(x_prompt, x_sample, state_conv_b, state_conv_c, state_pool, g_mix, w_in, a_ws, a_b, b_conv, c_conv,
           c_conv_b, c_ln_g, c_ln_b, d_w, d_scale, w_out, g_ffn, w_ff1, w_ff2, g_final):
    depth = g_mix.shape[0]
    bp, tp, d = x_prompt.shape
    bs, ts, _ = x_sample.shape
    rows_p, rows_s = bp * tp, bs * ts
    gw = GROUP_WIDTH
    ab_rows = jnp.broadcast_to(a_b[:, :, :, None], (depth, N_HEADS, A_CHUNK, HEAD_DIM))
    row = lambda p: p.reshape(depth, 1, gw)
    small = (a_ws, ab_rows, b_conv, c_conv, row(c_conv_b), row(c_ln_g), row(c_ln_b), d_w, row(d_scale))
    zb = jnp.zeros((depth, bp, B_KERNEL - 1, gw), x_prompt.dtype)
    zc = jnp.zeros((depth, bp, C_KERNEL - 1, gw), x_prompt.dtype)
    zd = jnp.zeros((depth, bp, POOL_HIST, gw), x_prompt.dtype)

    x, h = _gather_norm([x_prompt.reshape(rows_p, d), x_sample.reshape(rows_s, d)], g_mix[0])
    new_p, new_s, av_s = [], [], []
    ssq = None
    for l in range(depth):
        z = _matmul(h, w_in, l, out_dtype=jnp.float32, ssq=ssq, group_acts=Z_GROUP_ACTS, name="w_in",
                    **TILES["w_in"])
        outs_p = _mixers(z, zb, zc, zd, l, small, nseq=bp, seqlen=tp, row0=0, tm=A_CHUNK, pos0=0, want_av=False,
                         mix_prev=None)
        outs_s = _mixers(z, state_conv_b, state_conv_c, state_pool, l, small, nseq=bs, seqlen=ts, row0=rows_p,
                         tm=ts, pos0=PAST_LEN, want_av=True, mix_prev=outs_p[0])
        new_p.append(outs_p[1:4])
        new_s.append(outs_s[1:4])
        av_s.append(outs_s[4])
        x, xg, ssq = _matmul(outs_s[0], w_out, l, out_dtype=jnp.float32, res=x, norm_g=g_ffn[l], name="w_out",
                             **TILES["w_out"])
        u = _matmul(xg, w_ff1, l, out_dtype=jnp.bfloat16, relu2=True, ssq=ssq, name="w_ff1", **TILES["w_ff1"])
        if l + 1 < depth:
            x, h, ssq = _matmul(u, w_ff2, l, out_dtype=jnp.float32, res=x, norm_g=g_mix[l + 1], name="w_ff2",
                                **TILES["w_ff2"])
        else:
            x = _matmul(u, w_ff2, l, out_dtype=jnp.float32, res=x, name="w_ff2", **TILES["w_ff2"])
    y_p = _rmsnorm(x, g_final, jnp.float32, block0=0, nblocks=rows_p // NORM_ROWS).reshape(bp, tp, d)
    y_s = _rmsnorm(x, g_final, jnp.float32, block0=rows_p // NORM_ROWS, nblocks=rows_s // NORM_ROWS).reshape(bs, ts, d)
    stack = lambda per_layer, which: jnp.stack([o[which] for o in per_layer])
    return (y_p, y_s, stack(new_p, 0), stack(new_p, 1), stack(new_p, 2), stack(new_s, 0), stack(new_s, 1),
            stack(new_s, 2), jnp.stack(av_s))
```
